```python
import math
import jax, jax.numpy as jnp
from jax import lax
import numpy as np

D_MODEL = 1024
BATCH = 16
SEQ = 2048
DEPTH = 4

CTX_LEN = 256
GRID_W = 64
HEAD_DIM = 64
Q_BLOCK = 128
ROPE_THETA = 10000.0
ROPE_HALF = HEAD_DIM // 4
ATTN_SCALE = HEAD_DIM ** -0.5
A_HEADS = 8
A_KV_HEADS = 2
A_GROUP = A_HEADS // A_KV_HEADS
A_WIDTH = A_HEADS * HEAD_DIM
B_WIDTH = D_MODEL // 2
CONV_W = 3
C_HEADS = 4
C_WIDTH = C_HEADS * 2 * HEAD_DIM
SUBLN_DIM = 2 * HEAD_DIM
N_BRANCH = 3
EPS = 1e-6
PROJ_WIDTHS = (
    A_WIDTH, A_KV_HEADS * HEAD_DIM, A_KV_HEADS * HEAD_DIM, A_WIDTH,
    B_WIDTH, B_WIDTH, B_WIDTH, B_WIDTH,
    C_WIDTH, C_WIDTH, C_WIDTH, C_WIDTH,
    N_BRANCH * D_MODEL,
)
PROJ_TOTAL = sum(PROJ_WIDTHS)

kernel_name = "hybrid_gqa_shortconv_diffattn_prefix_dit"


def rmsnorm(x, g=None):
    xf = x.astype(jnp.float32)
    y = xf * lax.rsqrt(jnp.mean(xf * xf, axis=-1, keepdims=True) + EPS)
    if g is not None:
        y = y * g.astype(jnp.float32)
    return y.astype(x.dtype)


def split_proj(p):
    idx = np.cumsum(PROJ_WIDTHS)[:-1].tolist()
    return jnp.split(p, idx, axis=-1)


def axial_rope(x, cos, sin):
    shp = x.shape
    xr = x.astype(jnp.float32).reshape(*shp[:-1], 2, 2, ROPE_HALF)
    x1, x2 = xr[..., 0, :], xr[..., 1, :]
    bshape = (shp[1],) + (1,) * (x.ndim - 3) + (2, ROPE_HALF)
    cs, sn = cos.reshape(bshape), sin.reshape(bshape)
    out = jnp.stack([x1 * cs - x2 * sn, x2 * cs + x1 * sn], axis=-2)
    return out.reshape(shp).astype(x.dtype)


def sweep_query_blocks(fn, q):
    b, s = q.shape[:2]
    nb = s // Q_BLOCK
    qb = jnp.moveaxis(q.reshape(b, nb, Q_BLOCK, *q.shape[2:]), 1, 0)
    out = lax.map(fn, qb)
    return jnp.moveaxis(out, 0, 1).reshape(b, s, *out.shape[3:])


def gqa(q, k, v):
    s = jnp.einsum('bqkgd,bskd->bkgqs', q, k).astype(jnp.float32) * ATTN_SCALE
    p = jax.nn.softmax(s, axis=-1)
    return jnp.einsum('bkgqs,bskd->bqkgd', p.astype(v.dtype), v)


def diff_attention(q, k, v, lam):
    s = jnp.einsum('bqhmd,bshmd->bhmqs', q, k).astype(jnp.float32) * ATTN_SCALE
    p = jax.nn.softmax(s, axis=-1)
    a = p[:, :, 0] - lam * p[:, :, 1]
    return jnp.einsum('bhqs,bshe->bqhe', a.astype(v.dtype), v)


def short_conv(z, w):
    zp = jnp.pad(z, ((0, 0), (1, 1), (0, 0)))
    return w[0] * zp[:, :-2] + w[1] * zp[:, 1:-1] + w[2] * zp[:, 2:]


def merge_branches(ya, yb, yc, ag, bg, cg, mg, wa, wb, wc, wo):
    ga, gb, gc = jnp.split(jax.nn.sigmoid(mg), N_BRANCH, axis=-1)
    merged = (ga * ((ya * jax.nn.silu(ag)) @ wa)
              + gb * ((yb * jax.nn.silu(bg)) @ wb)
              + gc * ((yc * jax.nn.silu(cg)) @ wc))
    return merged @ wo


def setup_inputs(seed: int = 0) -> dict:
    key = jax.random.key(seed)
    ks = jax.random.split(key, 20)
    f32 = jnp.float32
    nrm = lambda k, shp, s: (jax.random.normal(k, shp, f32) * s).astype(f32)
    return {
        "x": nrm(ks[0], (BATCH, SEQ, D_MODEL), 1.0),
        "c": nrm(ks[1], (BATCH, D_MODEL), 1.0),
        "ctx": nrm(ks[2], (BATCH, CTX_LEN, D_MODEL), 1.0),
        "c_ctx": nrm(ks[3], (D_MODEL,), 1.0),
        "w_mod": nrm(ks[4], (DEPTH, D_MODEL, 3 * D_MODEL), 0.5 * D_MODEL ** -0.5),
        "b_mod": nrm(ks[5], (DEPTH, 3 * D_MODEL), 0.02),
        "w_in": nrm(ks[6], (DEPTH, D_MODEL, PROJ_TOTAL), D_MODEL ** -0.5),
        "q_norm_a": 1.0 + nrm(ks[7], (DEPTH, HEAD_DIM), 0.02),
        "k_norm_a": 1.0 + nrm(ks[8], (DEPTH, HEAD_DIM), 0.02),
        "conv_w_b": nrm(ks[9], (DEPTH, CONV_W, B_WIDTH), CONV_W ** -0.5),
        "lam_q1": nrm(ks[10], (DEPTH, HEAD_DIM), 0.1),
        "lam_k1": nrm(ks[11], (DEPTH, HEAD_DIM), 0.1),
        "lam_q2": nrm(ks[12], (DEPTH, HEAD_DIM), 0.1),
        "lam_k2": nrm(ks[13], (DEPTH, HEAD_DIM), 0.1),
        "subln_c": 1.0 + nrm(ks[14], (DEPTH, SUBLN_DIM), 0.02),
        "w_branch_a": nrm(ks[15], (DEPTH, A_WIDTH, D_MODEL), A_WIDTH ** -0.5),
        "w_branch_b": nrm(ks[16], (DEPTH, B_WIDTH, D_MODEL), B_WIDTH ** -0.5),
        "w_branch_c": nrm(ks[17], (DEPTH, C_WIDTH, D_MODEL), C_WIDTH ** -0.5),
        "w_out": nrm(ks[18], (DEPTH, D_MODEL, D_MODEL), D_MODEL ** -0.5),
        "final_norm": 1.0 + nrm(ks[19], (D_MODEL,), 0.02),
    }


def reference(x, c, ctx, c_ctx, w_mod, b_mod, w_in, q_norm_a, k_norm_a, conv_w_b,
              lam_q1, lam_k1, lam_q2, lam_k2, subln_c, w_branch_a, w_branch_b,
              w_branch_c, w_out, final_norm):
    B, S, D = x.shape
    L = ctx.shape[1]
    ROWS = S // GRID_W
    rows = jnp.repeat(jnp.arange(ROWS), GRID_W).astype(jnp.float32)
    cols = jnp.tile(jnp.arange(GRID_W), ROWS).astype(jnp.float32)
    inv_freq = ROPE_THETA ** (-jnp.arange(ROPE_HALF, dtype=jnp.float32) / ROPE_HALF)
    ang = jnp.stack([rows[:, None] * inv_freq, cols[:, None] * inv_freq], axis=1)
    cos, sin = jnp.cos(ang), jnp.sin(ang)

    for i in range(DEPTH):
        last = i == DEPTH - 1
        lambda_init = 0.8 - 0.6 * math.exp(-0.3 * i)
        shift, scale, gate = jnp.split(jax.nn.silu(c) @ w_mod[i] + b_mod[i], 3, axis=-1)
        shift_c, scale_c, gate_c = jnp.split(jax.nn.silu(c_ctx) @ w_mod[i] + b_mod[i], 3, axis=-1)
        hx = rmsnorm(x) * (1 + scale[:, None]) + shift[:, None]
        hc = rmsnorm(ctx) * (1 + scale_c) + shift_c
        (aq, ak, av, ag, bh, bb, bc, bg, cq, ck, cv, cg, mg) = split_proj(hx @ w_in[i])
        (aq_c, ak_c, av_c, ag_c, bh_c, bb_c, bc_c, bg_c,
         cq_c, ck_c, cv_c, cg_c, mg_c) = split_proj(hc @ w_in[i])

        qa = axial_rope(rmsnorm(aq.reshape(B, S, A_HEADS, HEAD_DIM), q_norm_a[i]), cos, sin)
        ka = axial_rope(rmsnorm(ak.reshape(B, S, A_KV_HEADS, HEAD_DIM), k_norm_a[i]), cos, sin)
        va = av.reshape(B, S, A_KV_HEADS, HEAD_DIM)
        ka_c = rmsnorm(ak_c.reshape(B, L, A_KV_HEADS, HEAD_DIM), k_norm_a[i])
        va_c = av_c.reshape(B, L, A_KV_HEADS, HEAD_DIM)
        ka_all = jnp.concatenate([ka_c, ka], axis=1)
        va_all = jnp.concatenate([va_c, va], axis=1)
        ya = sweep_query_blocks(lambda qb: gqa(qb, ka_all, va_all),
                                qa.reshape(B, S, A_KV_HEADS, A_GROUP, HEAD_DIM)).reshape(B, S, A_WIDTH)

        yb = bb * short_conv(bc * bh, conv_w_b[i])

        lam = (jnp.exp(jnp.sum(lam_q1[i].astype(jnp.float32) * lam_k1[i].astype(jnp.float32)))
               - jnp.exp(jnp.sum(lam_q2[i].astype(jnp.float32) * lam_k2[i].astype(jnp.float32)))
               + lambda_init)
        qc = axial_rope(cq.reshape(B, S, C_HEADS, 2, HEAD_DIM), cos, sin)
        kc = axial_rope(ck.reshape(B, S, C_HEADS, 2, HEAD_DIM), cos, sin)
        vc = cv.reshape(B, S, C_HEADS, SUBLN_DIM)
        kc_c = ck_c.reshape(B, L, C_HEADS, 2, HEAD_DIM)
        vc_c = cv_c.reshape(B, L, C_HEADS, SUBLN_DIM)
        kc_all = jnp.concatenate([kc_c, kc], axis=1)
        vc_all = jnp.concatenate([vc_c, vc], axis=1)
        yc = sweep_query_blocks(lambda qb: diff_attention(qb, kc_all, vc_all, lam), qc)
        yc = (rmsnorm(yc, subln_c[i]) * (1 - lambda_init)).reshape(B, S, C_WIDTH)

        out = merge_branches(ya, yb, yc, ag, bg, cg, mg,
                             w_branch_a[i], w_branch_b[i], w_branch_c[i], w_out[i])

        if not last:
            qa_c = rmsnorm(aq_c.reshape(B, L, A_HEADS, HEAD_DIM), q_norm_a[i])
            ya_c = gqa(qa_c.reshape(B, L, A_KV_HEADS, A_GROUP, HEAD_DIM), ka_c, va_c).reshape(B, L, A_WIDTH)
            yb_c = bb_c * short_conv(bc_c * bh_c, conv_w_b[i])
            yc_c = diff_attention(cq_c.reshape(B, L, C_HEADS, 2, HEAD_DIM), kc_c, vc_c, lam)
            yc_c = (rmsnorm(yc_c, subln_c[i]) * (1 - lambda_init)).reshape(B, L, C_WIDTH)
            out_c = merge_branches(ya_c, yb_c, yc_c, ag_c, bg_c, cg_c, mg_c,
                                   w_branch_a[i], w_branch_b[i], w_branch_c[i], w_out[i])
            ctx = ctx + gate_c * out_c

        x = x + gate[:, None] * out

    return rmsnorm(x, final_norm)
```

```python
import functools
import math

import jax
import jax.numpy as jnp
from jax import lax
from jax.experimental import pallas as pl
from jax.experimental.pallas import tpu as pltpu

F32 = jnp.float32
BF16 = jnp.bfloat16

HEAD_DIM = 64
GRID_W = 64
ROPE_HALF = HEAD_DIM // 4
ROPE_THETA = 10000.0
ATTN_SCALE = HEAD_DIM ** -0.5
A_HEADS = 8
A_KV_HEADS = 2
A_GROUP = A_HEADS // A_KV_HEADS
C_HEADS = 4
CONV_W = 3
EPS = 1e-6

LANES = 128
BF16_SUBLANES = 16
VMEM_LIMIT_BYTES = 56 * 1024 * 1024
ROW_TILE = 256
Q_TILE = 256


def _sigmoid(x):
    return 0.5 * jnp.tanh(0.5 * x) + 0.5


def _silu(x):
    return x * _sigmoid(x)


def _compiler_params(n_axes):
    return pltpu.CompilerParams(
        dimension_semantics=("arbitrary",) * n_axes,
        vmem_limit_bytes=VMEM_LIMIT_BYTES,
    )


def _mod_kernel(c_ref, w_ref, b_ref, o_ref):
    s = _silu(c_ref[...])
    o_ref[...] = jnp.dot(s.astype(BF16), w_ref[...], preferred_element_type=F32) + b_ref[...]


def _modulation(cond, w_mod, b_mod):
    depth, d, d3 = w_mod.shape
    rows = cond.shape[0]
    return pl.pallas_call(
        _mod_kernel,
        out_shape=jax.ShapeDtypeStruct((depth, rows, d3), F32),
        grid=(depth,),
        in_specs=[
            pl.BlockSpec((rows, d), lambda i: (0, 0)),
            pl.BlockSpec((None, d, d3), lambda i: (i, 0, 0)),
            pl.BlockSpec((None, 1, d3), lambda i: (i, 0, 0)),
        ],
        out_specs=pl.BlockSpec((None, rows, d3), lambda i: (i, 0, 0)),
        compiler_params=_compiler_params(1),
        name="modulation",
    )(cond, w_mod, b_mod)


def _rope(y, cos, sin_signed):
    width = y.shape[1]
    lane = lax.broadcasted_iota(jnp.int32, y.shape, 1)
    first = (lane % (2 * ROPE_HALF)) < ROPE_HALF
    partner = jnp.where(first, pltpu.roll(y, width - ROPE_HALF, 1), pltpu.roll(y, ROPE_HALF, 1))
    return y * cos + partner * sin_signed


def _head_rms(y, ones_blockdiag, gain):
    ss = jnp.dot((y * y).astype(BF16), ones_blockdiag, preferred_element_type=F32)
    return y * lax.rsqrt(ss * (1.0 / HEAD_DIM) + EPS) * gain


def _proj_kernel(x_ref, shift_ref, scale_ref, w_ref, qg_ref, kg_ref, cos_ref, sin_ref, ones_ref,
                 qa_ref, kd_ref, vat_ref, ga_ref, u_ref, bb_ref, gb_ref,
                 qc_ref, kc_ref, vct_ref, gc_ref, mg_ref, *, widths):
    aw, kvw, bw, cw, d = widths
    x = x_ref[...]
    ms = jnp.mean(x * x, axis=-1, keepdims=True)
    h = x * lax.rsqrt(ms + EPS) * (1.0 + scale_ref[...]) + shift_ref[...]
    hb = h.astype(BF16)

    def mm(lo, width):
        return jnp.dot(hb, w_ref[:, lo:lo + width], preferred_element_type=F32)

    cos = cos_ref[...]
    sin = sin_ref[...]
    cos_w = jnp.concatenate([cos] * (aw // LANES), axis=1)
    sin_w = jnp.concatenate([sin] * (aw // LANES), axis=1)

    col = 0
    q = _head_rms(mm(col, aw), ones_ref[...], qg_ref[...])
    qa_ref[...] = _rope(q, cos_w, sin_w).astype(BF16)
    col += aw
    kv = mm(col, 2 * kvw)
    k = _rope(_head_rms(kv[:, :kvw], ones_ref[:kvw, :kvw], kg_ref[...]), cos, sin)
    swapped = pltpu.roll(k, HEAD_DIM, 1)
    low = lax.broadcasted_iota(jnp.int32, k.shape, 1) < HEAD_DIM
    kd_ref[...] = jnp.concatenate(
        [jnp.where(low, k, swapped), jnp.where(low, swapped, k)], axis=1).astype(BF16)
    vat_ref[...] = kv[:, kvw:].T.astype(BF16)
    col += 2 * kvw
    ga_ref[...] = _silu(mm(col, aw)).astype(BF16)
    col += aw

    yb = mm(col, 4 * bw)
    u_ref[...] = (yb[:, 2 * bw:3 * bw] * yb[:, :bw]).astype(BF16)
    bb_ref[...] = yb[:, bw:2 * bw].astype(BF16)
    gb_ref[...] = _silu(yb[:, 3 * bw:]).astype(BF16)
    col += 4 * bw

    qc_ref[...] = (_rope(mm(col, cw), cos_w, sin_w) * ATTN_SCALE).astype(BF16)
    col += cw
    kc_ref[...] = _rope(mm(col, cw), cos_w, sin_w).astype(BF16)
    col += cw
    vct_ref[...] = mm(col, cw).T.astype(BF16)
    col += cw
    gc_ref[...] = _silu(mm(col, cw)).astype(BF16)
    col += cw

    mg_ref[...] = _sigmoid(mm(col, 3 * d)).astype(BF16)


def _mod_row_index(i, tiles_per_batch, ctx_tiles, ctx_row, which):
    bidx = jnp.where(i % tiles_per_batch < ctx_tiles, ctx_row, i // tiles_per_batch)
    return bidx * 3 + which


def _projection(xs, mods, w_in, qg, kg, cos_t, sin_t, ones_bd, *, batch, seq_all, ctx_len):
    n, d = xs.shape
    tm = ROW_TILE
    tpb = seq_all // tm
    ctx_tiles = ctx_len // tm
    aw = A_HEADS * HEAD_DIM
    kvw = A_KV_HEADS * HEAD_DIM
    bw = d // 2
    cw = C_HEADS * 2 * HEAD_DIM
    assert w_in.shape[1] == 2 * aw + 2 * kvw + 4 * bw + 4 * cw + 3 * d

    def mod_spec(which):
        return pl.BlockSpec(
            (None, 1, d), lambda i: (_mod_row_index(i, tpb, ctx_tiles, batch, which), 0, 0))

    def rows(width):
        return pl.BlockSpec((tm, width), lambda i: (i, 0))

    def const(shape):
        return pl.BlockSpec(shape, lambda i: (0,) * len(shape))

    def transposed(width):
        return pl.BlockSpec((None, width, tm), lambda i: (i // tpb, 0, i % tpb))

    def act(width):
        return jax.ShapeDtypeStruct((n, width), BF16)

    out_shapes = (
        act(aw), act(2 * kvw), jax.ShapeDtypeStruct((batch, kvw, seq_all), BF16), act(aw),
        act(bw), act(bw), act(bw),
        act(cw), act(cw), jax.ShapeDtypeStruct((batch, cw, seq_all), BF16), act(cw),
        act(3 * d),
    )
    out_specs = (
        rows(aw), rows(2 * kvw), transposed(kvw), rows(aw),
        rows(bw), rows(bw), rows(bw),
        rows(cw), rows(cw), transposed(cw), rows(cw),
        rows(3 * d),
    )
    return pl.pallas_call(
        functools.partial(_proj_kernel, widths=(aw, kvw, bw, cw, d)),
        out_shape=out_shapes,
        grid=(n // tm,),
        in_specs=[
            rows(d), mod_spec(0), mod_spec(1),
            pl.BlockSpec(w_in.shape, lambda i: (0, 0), pipeline_mode=pl.Buffered(1)),
            const(qg.shape), const(kg.shape),
            pl.BlockSpec((tm, LANES), lambda i: (i % tpb, 0)),
            pl.BlockSpec((tm, LANES), lambda i: (i % tpb, 0)),
            const(ones_bd.shape),
        ],
        out_specs=out_specs,
        compiler_params=_compiler_params(1),
        name="projection",
    )(xs, mods, mods, w_in, qg, kg, cos_t, sin_t, ones_bd)


def _softmax_pv(k, q_masked, v_aug):
    s = lax.dot_general(k, q_masked, (((1,), (1,)), ((), ())), preferred_element_type=F32)
    m = jnp.max(s, axis=0, keepdims=True)
    p = jnp.exp(s - m).astype(BF16)
    return jnp.dot(v_aug, p, preferred_element_type=F32)


def _half_mask(q, keep_low):
    low = lax.broadcasted_iota(jnp.int32, q.shape, 1) < HEAD_DIM
    return jnp.where(low == keep_low, q, jnp.zeros_like(q))


def _gqa_kernel(q_ref, k_ref, vt_ref, o_ref, vaug_ref, *, ctx_len):
    j = pl.program_id(2)

    @pl.when(j == 0)
    def _():
        vaug_ref[:HEAD_DIM, :] = vt_ref[...]
        vaug_ref[HEAD_DIM:, :] = jnp.ones((BF16_SUBLANES, vaug_ref.shape[1]), BF16)

    def attend(n_keys):
        k = k_ref[:n_keys, :]
        v_aug = vaug_ref[:, :n_keys]
        outs = []
        for hh in range(A_GROUP):
            pair = q_ref[:, (hh // 2) * LANES:(hh // 2 + 1) * LANES]
            oa = _softmax_pv(k, _half_mask(pair, hh % 2 == 0), v_aug)
            outs.append(oa[:HEAD_DIM] * (1.0 / oa[HEAD_DIM:HEAD_DIM + 1]))
        o_ref[...] = jnp.concatenate(outs, axis=0).T.astype(BF16)

    @pl.when(j == 0)
    def _():
        attend(ctx_len)

    @pl.when(j != 0)
    def _():
        attend(k_ref.shape[0])


def _gqa_attention(qa, kd, vat, *, batch, seq_all, ctx_len):
    n = qa.shape[0]
    tq = Q_TILE
    nq = seq_all // tq
    gw = A_GROUP * HEAD_DIM
    return pl.pallas_call(
        functools.partial(_gqa_kernel, ctx_len=ctx_len),
        out_shape=jax.ShapeDtypeStruct((n, A_HEADS * HEAD_DIM), BF16),
        grid=(batch, A_KV_HEADS, nq),
        in_specs=[
            pl.BlockSpec((tq, gw), lambda b, g, j: (b * nq + j, g)),
            pl.BlockSpec((seq_all, 2 * HEAD_DIM), lambda b, g, j: (b, g)),
            pl.BlockSpec((None, HEAD_DIM, seq_all), lambda b, g, j: (b, g, 0)),
        ],
        out_specs=pl.BlockSpec((tq, gw), lambda b, g, j: (b * nq + j, g)),
        scratch_shapes=[pltpu.VMEM((HEAD_DIM + BF16_SUBLANES, seq_all), BF16)],
        compiler_params=_compiler_params(3),
        name="gqa_attention",
    )(qa, kd, vat)


def _diff_kernel(q_ref, k_ref, vt_ref, lq1_ref, lk1_ref, lq2_ref, lk2_ref, subln_ref,
                 o_ref, vaug_ref, *, ctx_len, lambda_init):
    j = pl.program_id(2)
    vdim = 2 * HEAD_DIM

    @pl.when(j == 0)
    def _():
        vaug_ref[:vdim, :] = vt_ref[...]
        vaug_ref[vdim:, :] = jnp.ones((BF16_SUBLANES, vaug_ref.shape[1]), BF16)

    lam = (jnp.exp(jnp.sum(lq1_ref[...] * lk1_ref[...], axis=-1, keepdims=True))
           - jnp.exp(jnp.sum(lq2_ref[...] * lk2_ref[...], axis=-1, keepdims=True))
           + lambda_init)

    def attend(n_keys):
        k = k_ref[:n_keys, :]
        v_aug = vaug_ref[:, :n_keys]
        q = q_ref[...]
        o1 = _softmax_pv(k, _half_mask(q, True), v_aug)
        o2 = _softmax_pv(k, _half_mask(q, False), v_aug)
        o = (o1[:vdim] * (1.0 / o1[vdim:vdim + 1])
             - lam * (o2[:vdim] * (1.0 / o2[vdim:vdim + 1])))
        ms = jnp.mean(o * o, axis=0, keepdims=True)
        y = (o * lax.rsqrt(ms + EPS)).T
        o_ref[...] = (y * (subln_ref[...] * (1.0 - lambda_init))).astype(BF16)

    @pl.when(j == 0)
    def _():
        attend(ctx_len)

    @pl.when(j != 0)
    def _():
        attend(k_ref.shape[0])


def _diff_attention(qc, kc, vct, lq1, lk1, lq2, lk2, subln, *, batch, seq_all, ctx_len, lambda_init):
    n = qc.shape[0]
    tq = Q_TILE
    nq = seq_all // tq
    vdim = 2 * HEAD_DIM

    def vec(a):
        return pl.BlockSpec(a.shape, lambda b, h, j: (0, 0))

    return pl.pallas_call(
        functools.partial(_diff_kernel, ctx_len=ctx_len, lambda_init=lambda_init),
        out_shape=jax.ShapeDtypeStruct((n, C_HEADS * vdim), BF16),
        grid=(batch, C_HEADS, nq),
        in_specs=[
            pl.BlockSpec((tq, vdim), lambda b, h, j: (b * nq + j, h)),
            pl.BlockSpec((seq_all, vdim), lambda b, h, j: (b, h)),
            pl.BlockSpec((None, vdim, seq_all), lambda b, h, j: (b, h, 0)),
            vec(lq1), vec(lk1), vec(lq2), vec(lk2), vec(subln),
        ],
        out_specs=pl.BlockSpec((tq, vdim), lambda b, h, j: (b * nq + j, h)),
        scratch_shapes=[pltpu.VMEM((vdim + BF16_SUBLANES, seq_all), BF16)],
        compiler_params=_compiler_params(3),
        name="diff_attention",
    )(qc, kc, vct, lq1, lk1, lq2, lk2, subln)


def _merge_kernel(x_ref, gate_ref, ya_ref, ga_ref, u_ref, uprev_ref, unext_ref, bb_ref, gb_ref,
                  yc_ref, gc_ref, mg_ref, cw_ref, wa_ref, wb_ref, wc_ref, wo_ref, fn_ref,
                  o_ref, *, tiles_per_batch, ctx_tiles, first_tile, final):
    tm, d = x_ref.shape
    i = pl.program_id(0)
    if final:
        lat_tiles = tiles_per_batch - ctx_tiles
        t = ctx_tiles + i % lat_tiles
    else:
        t = i % tiles_per_batch
    del first_tile
    has_prev = jnp.logical_and(t != 0, t != ctx_tiles)
    has_next = jnp.logical_and(t != ctx_tiles - 1, t != tiles_per_batch - 1)

    za = ya_ref[...] * ga_ref[...]
    pa = jnp.dot(za, wa_ref[...], preferred_element_type=F32)

    u = u_ref[...].astype(F32)
    prev_row = jnp.where(has_prev, uprev_ref[BF16_SUBLANES - 1:, :].astype(F32), 0.0)
    next_row = jnp.where(has_next, unext_ref[:1, :].astype(F32), 0.0)
    row = lax.broadcasted_iota(jnp.int32, u.shape, 0)
    u_before = jnp.where(row == 0, prev_row, pltpu.roll(u, 1, 0))
    u_after = jnp.where(row == tm - 1, next_row, pltpu.roll(u, tm - 1, 0))
    cw = cw_ref[...]
    conv = cw[0:1] * u_before + cw[1:2] * u + cw[2:3] * u_after
    zb = (bb_ref[...].astype(F32) * conv).astype(BF16) * gb_ref[...]
    pb = jnp.dot(zb, wb_ref[...], preferred_element_type=F32)

    zc = yc_ref[...] * gc_ref[...]
    pc = jnp.dot(zc, wc_ref[...], preferred_element_type=F32)

    merged = (mg_ref[:, :d].astype(F32) * pa
              + mg_ref[:, d:2 * d].astype(F32) * pb
              + mg_ref[:, 2 * d:].astype(F32) * pc)
    out = jnp.dot(merged.astype(BF16), wo_ref[...], preferred_element_type=F32)
    xn = x_ref[...] + gate_ref[...] * out
    if final:
        ms = jnp.mean(xn * xn, axis=-1, keepdims=True)
        xn = xn * lax.rsqrt(ms + EPS) * fn_ref[...]
    o_ref[...] = xn


def _merge(xs, mods, ya, ga, u, bb, gb, yc, gc, mg, conv_w, wa, wb, wc, wo, final_norm,
           *, batch, seq_all, ctx_len, final):
    n, d = xs.shape
    tm = ROW_TILE
    tpb = seq_all // tm
    ctx_tiles = ctx_len // tm
    lat_tiles = tpb - ctx_tiles
    halo_per_tile = tm // BF16_SUBLANES
    n_halo = n // BF16_SUBLANES

    if final:
        n_steps = batch * lat_tiles

        def tile(i):
            return (i // lat_tiles) * tpb + ctx_tiles + i % lat_tiles

        def bidx(i):
            return i // lat_tiles
        out_rows = batch * lat_tiles * tm
    else:
        n_steps = n // tm

        def tile(i):
            return i

        def bidx(i):
            return jnp.where(i % tpb < ctx_tiles, batch, i // tpb)
        out_rows = n

    def rows(width):
        return pl.BlockSpec((tm, width), lambda i: (tile(i), 0))

    def const(a):
        return pl.BlockSpec(a.shape, lambda i: (0,) * a.ndim)

    bw = u.shape[1]
    in_specs = [
        rows(d),
        pl.BlockSpec((None, 1, d), lambda i: (bidx(i) * 3 + 2, 0, 0)),
        rows(ya.shape[1]), rows(ga.shape[1]),
        rows(bw),
        pl.BlockSpec((BF16_SUBLANES, bw), lambda i: (jnp.maximum(tile(i) * halo_per_tile - 1, 0), 0)),
        pl.BlockSpec((BF16_SUBLANES, bw),
                     lambda i: (jnp.minimum((tile(i) + 1) * halo_per_tile, n_halo - 1), 0)),
        rows(bw), rows(bw),
        rows(yc.shape[1]), rows(gc.shape[1]),
        rows(mg.shape[1]),
        const(conv_w), const(wa), const(wb), const(wc), const(wo), const(final_norm),
    ]
    kwargs = {}
    if not final:
        kwargs["input_output_aliases"] = {0: 0}
    return pl.pallas_call(
        functools.partial(_merge_kernel, tiles_per_batch=tpb, ctx_tiles=ctx_tiles,
                          first_tile=0, final=final),
        out_shape=jax.ShapeDtypeStruct((out_rows, d), F32),
        grid=(n_steps,),
        in_specs=in_specs,
        out_specs=pl.BlockSpec((tm, d), lambda i: (i, 0)),
        compiler_params=_compiler_params(1),
        name="merge_final" if final else "merge",
        **kwargs,
    )(xs, mods, ya, ga, u, u, u, bb, gb, yc, gc, mg, conv_w, wa, wb, wc, wo, final_norm)


def _rope_tables(seq, ctx_len):
    pos = jnp.arange(seq)
    rows = (pos // GRID_W).astype(F32)
    cols = (pos % GRID_W).astype(F32)
    inv_freq = ROPE_THETA ** (-jnp.arange(ROPE_HALF, dtype=F32) / ROPE_HALF)
    lane = jnp.arange(LANES)
    d64 = lane % HEAD_DIM
    axis = d64 // (2 * ROPE_HALF)
    freq = inv_freq[d64 % ROPE_HALF]
    ang = jnp.where(axis[None, :] == 0, rows[:, None], cols[:, None]) * freq[None, :]
    sign = jnp.where((d64 % (2 * ROPE_HALF)) < ROPE_HALF, -1.0, 1.0).astype(F32)
    cos = jnp.concatenate([jnp.ones((ctx_len, LANES), F32), jnp.cos(ang)], axis=0)
    sin = jnp.concatenate([jnp.zeros((ctx_len, LANES), F32), jnp.sin(ang) * sign[None, :]], axis=0)
    return cos, sin


def kernel(x, c, ctx, c_ctx, w_mod, b_mod, w_in, q_norm_a, k_norm_a, conv_w_b, lam_q1, lam_k1,
           lam_q2, lam_k2, subln_c, w_branch_a, w_branch_b, w_branch_c, w_out, final_norm):
    batch, seq, d = x.shape
    ctx_len = ctx.shape[1]
    depth = w_mod.shape[0]
    seq_all = ctx_len + seq
    n = batch * seq_all
    assert ctx_len % ROW_TILE == 0 and seq % ROW_TILE == 0
    assert ctx_len == Q_TILE and seq % Q_TILE == 0

    mod_rows = -(-(batch + 1) // 8) * 8
    cond = jnp.concatenate(
        [c, c_ctx[None, :], jnp.zeros((mod_rows - batch - 1, d), F32)], axis=0)
    mods = _modulation(cond, w_mod.astype(BF16), b_mod[:, None, :])
    mods = mods.reshape(depth, mod_rows * 3, 1, d)

    cos_t, sin_t = _rope_tables(seq, ctx_len)
    aw = A_HEADS * HEAD_DIM
    head_of_lane = jnp.arange(aw) // HEAD_DIM
    ones_bd = (head_of_lane[:, None] == head_of_lane[None, :]).astype(BF16)

    xs = jnp.concatenate([ctx, x], axis=1).reshape(n, d)
    dims = dict(batch=batch, seq_all=seq_all, ctx_len=ctx_len)
    out = None
    for i in range(depth):
        last = i == depth - 1
        lambda_init = 0.8 - 0.6 * math.exp(-0.3 * i)
        qg = (jnp.tile(q_norm_a[i], A_HEADS) * ATTN_SCALE)[None, :]
        kg = jnp.tile(k_norm_a[i], A_KV_HEADS)[None, :]
        (qa, kd, vat, ga, u, bb, gb, qc, kc, vct, gc, mg) = _projection(
            xs, mods[i], w_in[i].astype(BF16), qg, kg, cos_t, sin_t, ones_bd, **dims)
        ya = _gqa_attention(qa, kd, vat, **dims)
        yc = _diff_attention(qc, kc, vct, lam_q1[i][None, :], lam_k1[i][None, :],
                             lam_q2[i][None, :], lam_k2[i][None, :], subln_c[i][None, :],
                             lambda_init=lambda_init, **dims)
        res = _merge(xs, mods[i], ya, ga, u, bb, gb, yc, gc, mg, conv_w_b[i],
                     w_branch_a[i].astype(BF16), w_branch_b[i].astype(BF16),
                     w_branch_c[i].astype(BF16), w_out[i].astype(BF16), final_norm[None, :],
                     final=last, **dims)
        if last:
            out = res.reshape(batch, seq, d)
        else:
            xs = res
    return out
```

```python
import functools
import math

import jax
import jax.numpy as jnp
from jax import lax
from jax.experimental import pallas as pl
from jax.experimental.pallas import tpu as pltpu

F32 = jnp.float32
BF16 = jnp.bfloat16

HEAD_DIM = 64
GRID_W = 64
ROPE_HALF = HEAD_DIM // 4
ROPE_THETA = 10000.0
ATTN_SCALE = HEAD_DIM ** -0.5
LOG2_E = math.log2(math.e)
A_HEADS = 8
A_KV_HEADS = 2
A_GROUP = A_HEADS // A_KV_HEADS
C_HEADS = 4
CONV_W = 3
EPS = 1e-6

LANES = 128
BF16_SUBLANES = 16
VMEM_LIMIT_BYTES = 56 * 1024 * 1024
ROW_TILE = 256
Q_TILE = 256


def _sigmoid(x):
    return 0.5 * jnp.tanh(0.5 * x) + 0.5


def _silu(x):
    return x * _sigmoid(x)


def _compiler_params(n_axes):
    return pltpu.CompilerParams(
        dimension_semantics=("arbitrary",) * n_axes,
        vmem_limit_bytes=VMEM_LIMIT_BYTES,
    )


def _mod_kernel(c_ref, w_ref, b_ref, o_ref):
    s = _silu(c_ref[...])
    o_ref[...] = jnp.dot(s.astype(BF16), w_ref[...], preferred_element_type=F32) + b_ref[...]


def _modulation(cond, w_mod, b_mod):
    depth, d, d3 = w_mod.shape
    rows = cond.shape[0]
    return pl.pallas_call(
        _mod_kernel,
        out_shape=jax.ShapeDtypeStruct((depth, rows, d3), F32),
        grid=(depth,),
        in_specs=[
            pl.BlockSpec((rows, d), lambda i: (0, 0)),
            pl.BlockSpec((None, d, d3), lambda i: (i, 0, 0)),
            pl.BlockSpec((None, 1, d3), lambda i: (i, 0, 0)),
        ],
        out_specs=pl.BlockSpec((None, rows, d3), lambda i: (i, 0, 0)),
        compiler_params=_compiler_params(1),
        name="modulation",
    )(cond, w_mod, b_mod)


def _rope(y, cos, sin_signed):
    width = y.shape[1]
    lane = lax.broadcasted_iota(jnp.int32, y.shape, 1)
    first = (lane % (2 * ROPE_HALF)) < ROPE_HALF
    partner = jnp.where(first, pltpu.roll(y, width - ROPE_HALF, 1), pltpu.roll(y, ROPE_HALF, 1))
    return y * cos + partner * sin_signed


def _head_rms(y, ones_blockdiag, gain):
    ss = jnp.dot((y * y).astype(BF16), ones_blockdiag, preferred_element_type=F32)
    return y * lax.rsqrt(ss * (1.0 / HEAD_DIM) + EPS) * gain


def _proj_kernel(x_ref, shift_ref, scale_ref, w_ref, qg_ref, kg_ref, cos_ref, sin_ref, ones_ref,
                 qa_ref, kd_ref, vat_ref, ga_ref, u_ref, bb_ref, gb_ref,
                 qc_ref, kc_ref, vct_ref, gc_ref, mg_ref, *, widths):
    aw, kvw, bw, cw, d = widths
    x = x_ref[...]
    ms = jnp.mean(x * x, axis=-1, keepdims=True)
    h = x * lax.rsqrt(ms + EPS) * (1.0 + scale_ref[...]) + shift_ref[...]
    hb = h.astype(BF16)

    def mm(lo, width):
        return jnp.dot(hb, w_ref[:, lo:lo + width], preferred_element_type=F32)

    cos = cos_ref[...]
    sin = sin_ref[...]
    cos_w = jnp.concatenate([cos] * (aw // LANES), axis=1)
    sin_w = jnp.concatenate([sin] * (aw // LANES), axis=1)

    col = 0
    q = _head_rms(mm(col, aw), ones_ref[...], qg_ref[...])
    qa_ref[...] = _rope(q, cos_w, sin_w).astype(BF16)
    col += aw
    kv = mm(col, 2 * kvw)
    k = _rope(_head_rms(kv[:, :kvw], ones_ref[:kvw, :kvw], kg_ref[...]), cos, sin)
    swapped = pltpu.roll(k, HEAD_DIM, 1)
    low = lax.broadcasted_iota(jnp.int32, k.shape, 1) < HEAD_DIM
    kd_ref[...] = jnp.concatenate(
        [jnp.where(low, k, swapped), jnp.where(low, swapped, k)], axis=1).astype(BF16)
    vat_ref[...] = kv[:, kvw:].T.astype(BF16)
    col += 2 * kvw
    ga_ref[...] = _silu(mm(col, aw)).astype(BF16)
    col += aw

    yb = mm(col, 4 * bw)
    u_ref[...] = (yb[:, 2 * bw:3 * bw] * yb[:, :bw]).astype(BF16)
    bb_ref[...] = yb[:, bw:2 * bw].astype(BF16)
    gb_ref[...] = _silu(yb[:, 3 * bw:]).astype(BF16)
    col += 4 * bw

    qc_ref[...] = (_rope(mm(col, cw), cos_w, sin_w) * (ATTN_SCALE * LOG2_E)).astype(BF16)
    col += cw
    kc_ref[...] = _rope(mm(col, cw), cos_w, sin_w).astype(BF16)
    col += cw
    vct_ref[...] = mm(col, cw).T.astype(BF16)
    col += cw
    gc_ref[...] = _silu(mm(col, cw)).astype(BF16)
    col += cw

    mg_ref[...] = _sigmoid(mm(col, 3 * d)).astype(BF16)


def _mod_row_index(i, tiles_per_batch, ctx_tiles, ctx_row, which):
    bidx = jnp.where(i % tiles_per_batch < ctx_tiles, ctx_row, i // tiles_per_batch)
    return bidx * 3 + which


def _projection(xs, mods, w_in, qg, kg, cos_t, sin_t, ones_bd, *, batch, seq_all, ctx_len):
    n, d = xs.shape
    tm = ROW_TILE
    tpb = seq_all // tm
    ctx_tiles = ctx_len // tm
    aw = A_HEADS * HEAD_DIM
    kvw = A_KV_HEADS * HEAD_DIM
    bw = d // 2
    cw = C_HEADS * 2 * HEAD_DIM
    assert w_in.shape[1] == 2 * aw + 2 * kvw + 4 * bw + 4 * cw + 3 * d

    def mod_spec(which):
        return pl.BlockSpec(
            (None, 1, d), lambda i: (_mod_row_index(i, tpb, ctx_tiles, batch, which), 0, 0))

    def rows(width):
        return pl.BlockSpec((tm, width), lambda i: (i, 0))

    def const(shape):
        return pl.BlockSpec(shape, lambda i: (0,) * len(shape))

    def transposed(width):
        return pl.BlockSpec((None, width, tm), lambda i: (i // tpb, 0, i % tpb))

    def act(width):
        return jax.ShapeDtypeStruct((n, width), BF16)

    out_shapes = (
        act(aw), act(2 * kvw), jax.ShapeDtypeStruct((batch, kvw, seq_all), BF16), act(aw),
        act(bw), act(bw), act(bw),
        act(cw), act(cw), jax.ShapeDtypeStruct((batch, cw, seq_all), BF16), act(cw),
        act(3 * d),
    )
    out_specs = (
        rows(aw), rows(2 * kvw), transposed(kvw), rows(aw),
        rows(bw), rows(bw), rows(bw),
        rows(cw), rows(cw), transposed(cw), rows(cw),
        rows(3 * d),
    )
    return pl.pallas_call(
        functools.partial(_proj_kernel, widths=(aw, kvw, bw, cw, d)),
        out_shape=out_shapes,
        grid=(n // tm,),
        in_specs=[
            rows(d), mod_spec(0), mod_spec(1),
            pl.BlockSpec(w_in.shape, lambda i: (0, 0), pipeline_mode=pl.Buffered(1)),
            const(qg.shape), const(kg.shape),
            pl.BlockSpec((tm, LANES), lambda i: (i % tpb, 0)),
            pl.BlockSpec((tm, LANES), lambda i: (i % tpb, 0)),
            const(ones_bd.shape),
        ],
        out_specs=out_specs,
        compiler_params=_compiler_params(1),
        name="projection",
    )(xs, mods, mods, w_in, qg, kg, cos_t, sin_t, ones_bd)


def _split_halves(q):
    low = lax.broadcasted_iota(jnp.int32, q.shape, 1) < HEAD_DIM
    zero = jnp.zeros_like(q)
    return jnp.concatenate([jnp.where(low, q, zero), jnp.where(low, zero, q)], axis=0)


def _scores(k, q_pair, s_ref, m_ref):
    s = lax.dot_general(k, _split_halves(q_pair), (((1,), (1,)), ((), ())),
                        preferred_element_type=F32)
    s_ref[...] = s
    m_ref[...] = jnp.max(s, axis=0, keepdims=True)


def _softmax_pv(s_ref, m_ref, v_aug, vdim):
    p = jnp.exp2(s_ref[...] - m_ref[...]).astype(BF16)
    oa = jnp.dot(v_aug, p, preferred_element_type=F32)
    return oa[:vdim] * (1.0 / oa[vdim:vdim + 1])


def _fill_v_aug(vaug_ref, vt_ref, vdim):
    vaug_ref[:vdim, :] = vt_ref[...]
    vaug_ref[vdim:, :] = jnp.ones((BF16_SUBLANES, vaug_ref.shape[1]), BF16)


def _gqa_kernel(q_ref, k_ref, vt_ref, o_ref, vaug_ref, s0_ref, s1_ref, m0_ref, m1_ref, ot_ref,
                sc_ref, mc_ref, *, ctx_len, tq):
    seq_all = k_ref.shape[0]
    n_lat = (seq_all - ctx_len) // tq
    _fill_v_aug(vaug_ref, vt_ref, HEAD_DIM)

    def q_pair(row0, pair):
        return q_ref[pl.ds(row0, tq), pair * LANES:(pair + 1) * LANES]

    def put(pair, o):
        ot_ref[(2 * pair) * HEAD_DIM:(2 * pair + 1) * HEAD_DIM, :] = o[:, :tq]
        ot_ref[(2 * pair + 1) * HEAD_DIM:(2 * pair + 2) * HEAD_DIM, :] = o[:, tq:]

    for pair in range(A_GROUP // 2):
        _scores(k_ref[:ctx_len, :], q_pair(0, pair), sc_ref, mc_ref)
        put(pair, _softmax_pv(sc_ref, mc_ref, vaug_ref[:, :ctx_len], HEAD_DIM))
    o_ref[:ctx_len, :] = ot_ref[...].T.astype(BF16)

    k = k_ref[...]
    first = ctx_len // tq
    _scores(k, q_pair(ctx_len, 0), s0_ref, m0_ref)

    def body(t, carry):
        row0 = pl.multiple_of(t * tq, tq)
        nxt = pl.multiple_of(jnp.minimum(t + 1, first + n_lat - 1) * tq, tq)
        _scores(k, q_pair(row0, 1), s1_ref, m1_ref)
        put(0, _softmax_pv(s0_ref, m0_ref, vaug_ref[...], HEAD_DIM))
        _scores(k, q_pair(nxt, 0), s0_ref, m0_ref)
        put(1, _softmax_pv(s1_ref, m1_ref, vaug_ref[...], HEAD_DIM))
        o_ref[pl.ds(row0, tq), :] = ot_ref[...].T.astype(BF16)
        return carry

    lax.fori_loop(first, first + n_lat, body, 0)


def _gqa_attention(qa, kd, vat, *, batch, seq_all, ctx_len):
    n = qa.shape[0]
    tq = Q_TILE
    gw = A_GROUP * HEAD_DIM
    assert A_GROUP == 4 and ctx_len == tq
    return pl.pallas_call(
        functools.partial(_gqa_kernel, ctx_len=ctx_len, tq=tq),
        out_shape=jax.ShapeDtypeStruct((n, A_HEADS * HEAD_DIM), BF16),
        grid=(batch, A_KV_HEADS),
        in_specs=[
            pl.BlockSpec((seq_all, gw), lambda b, g: (b, g)),
            pl.BlockSpec((seq_all, 2 * HEAD_DIM), lambda b, g: (b, g)),
            pl.BlockSpec((None, HEAD_DIM, seq_all), lambda b, g: (b, g, 0)),
        ],
        out_specs=pl.BlockSpec((seq_all, gw), lambda b, g: (b, g)),
        scratch_shapes=[
            pltpu.VMEM((HEAD_DIM + BF16_SUBLANES, seq_all), BF16),
            pltpu.VMEM((seq_all, 2 * tq), F32), pltpu.VMEM((seq_all, 2 * tq), F32),
            pltpu.VMEM((1, 2 * tq), F32), pltpu.VMEM((1, 2 * tq), F32),
            pltpu.VMEM((gw, tq), F32),
            pltpu.VMEM((ctx_len, 2 * tq), F32), pltpu.VMEM((1, 2 * tq), F32),
        ],
        compiler_params=_compiler_params(2),
        name="gqa_attention",
    )(qa, kd, vat)


def _diff_kernel(q_ref, k_ref, vt_ref, lq1_ref, lk1_ref, lq2_ref, lk2_ref, subln_ref,
                 o_ref, vaug_ref, s0_ref, s1_ref, m0_ref, m1_ref, sc_ref, mc_ref,
                 *, ctx_len, tq, lambda_init):
    seq_all = k_ref.shape[0]
    n_lat = (seq_all - ctx_len) // tq
    vdim = 2 * HEAD_DIM
    _fill_v_aug(vaug_ref, vt_ref, vdim)

    lam = (jnp.exp(jnp.sum(lq1_ref[...] * lk1_ref[...], axis=-1, keepdims=True))
           - jnp.exp(jnp.sum(lq2_ref[...] * lk2_ref[...], axis=-1, keepdims=True))
           + lambda_init)
    out_gain = subln_ref[...] * (1.0 - lambda_init)

    def finish(row0, o):
        dlt = o[:, :tq] - lam * o[:, tq:]
        ms = jnp.mean(dlt * dlt, axis=0, keepdims=True)
        y = (dlt * lax.rsqrt(ms + EPS)).T
        o_ref[pl.ds(row0, tq), :] = (y * out_gain).astype(BF16)

    _scores(k_ref[:ctx_len, :], q_ref[:tq, :], sc_ref, mc_ref)
    finish(0, _softmax_pv(sc_ref, mc_ref, vaug_ref[:, :ctx_len], vdim))

    k = k_ref[...]
    first = ctx_len // tq
    last = first + n_lat - 1
    _scores(k, q_ref[pl.ds(ctx_len, tq), :], s0_ref, m0_ref)

    def body(it, carry):
        row_a = pl.multiple_of((first + 2 * it) * tq, tq)
        row_b = pl.multiple_of((first + 2 * it + 1) * tq, tq)
        row_n = pl.multiple_of(jnp.minimum(first + 2 * it + 2, last) * tq, tq)
        _scores(k, q_ref[pl.ds(row_b, tq), :], s1_ref, m1_ref)
        finish(row_a, _softmax_pv(s0_ref, m0_ref, vaug_ref[...], vdim))
        _scores(k, q_ref[pl.ds(row_n, tq), :], s0_ref, m0_ref)
        finish(row_b, _softmax_pv(s1_ref, m1_ref, vaug_ref[...], vdim))
        return carry

    lax.fori_loop(0, n_lat // 2, body, 0)


def _diff_attention(qc, kc, vct, lq1, lk1, lq2, lk2, subln, *, batch, seq_all, ctx_len, lambda_init):
    n = qc.shape[0]
    tq = Q_TILE
    vdim = 2 * HEAD_DIM
    assert ctx_len == tq and ((seq_all - ctx_len) // tq) % 2 == 0

    def vec(a):
        return pl.BlockSpec(a.shape, lambda b, h: (0, 0))

    return pl.pallas_call(
        functools.partial(_diff_kernel, ctx_len=ctx_len, tq=tq, lambda_init=lambda_init),
        out_shape=jax.ShapeDtypeStruct((n, C_HEADS * vdim), BF16),
        grid=(batch, C_HEADS),
        in_specs=[
            pl.BlockSpec((seq_all, vdim), lambda b, h: (b, h)),
            pl.BlockSpec((seq_all, vdim), lambda b, h: (b, h)),
            pl.BlockSpec((None, vdim, seq_all), lambda b, h: (b, h, 0)),
            vec(lq1), vec(lk1), vec(lq2), vec(lk2), vec(subln),
        ],
        out_specs=pl.BlockSpec((seq_all, vdim), lambda b, h: (b, h)),
        scratch_shapes=[
            pltpu.VMEM((vdim + BF16_SUBLANES, seq_all), BF16),
            pltpu.VMEM((seq_all, 2 * tq), F32), pltpu.VMEM((seq_all, 2 * tq), F32),
            pltpu.VMEM((1, 2 * tq), F32), pltpu.VMEM((1, 2 * tq), F32),
            pltpu.VMEM((ctx_len, 2 * tq), F32), pltpu.VMEM((1, 2 * tq), F32),
        ],
        compiler_params=_compiler_params(2),
        name="diff_attention",
    )(qc, kc, vct, lq1, lk1, lq2, lk2, subln)


def _merge_kernel(x_ref, gate_ref, ya_ref, ga_ref, u_ref, uprev_ref, unext_ref, bb_ref, gb_ref,
                  yc_ref, gc_ref, mg_ref, cw_ref, wa_ref, wb_ref, wc_ref, wo_ref, fn_ref,
                  o_ref, *, tiles_per_batch, ctx_tiles, final):
    tm, d = x_ref.shape
    i = pl.program_id(0)
    if final:
        t = ctx_tiles + i % (tiles_per_batch - ctx_tiles)
    else:
        t = i % tiles_per_batch
    has_prev = jnp.logical_and(t != 0, t != ctx_tiles)
    has_next = jnp.logical_and(t != ctx_tiles - 1, t != tiles_per_batch - 1)

    za = ya_ref[...] * ga_ref[...]
    pa = jnp.dot(za, wa_ref[...], preferred_element_type=F32)

    u = u_ref[...].astype(F32)
    prev_row = jnp.where(has_prev, uprev_ref[...].astype(F32)[BF16_SUBLANES - 1:], 0.0)
    next_row = jnp.where(has_next, unext_ref[...].astype(F32)[:1], 0.0)
    row = lax.broadcasted_iota(jnp.int32, u.shape, 0)
    u_before = jnp.where(row == 0, prev_row, pltpu.roll(u, 1, 0))
    u_after = jnp.where(row == tm - 1, next_row, pltpu.roll(u, tm - 1, 0))
    cw = cw_ref[...]
    conv = cw[0:1] * u_before + cw[1:2] * u + cw[2:3] * u_after
    zb = (bb_ref[...].astype(F32) * conv).astype(BF16) * gb_ref[...]
    pb = jnp.dot(zb, wb_ref[...], preferred_element_type=F32)

    zc = yc_ref[...] * gc_ref[...]
    pc = jnp.dot(zc, wc_ref[...], preferred_element_type=F32)

    merged = (mg_ref[:, :d].astype(F32) * pa
              + mg_ref[:, d:2 * d].astype(F32) * pb
              + mg_ref[:, 2 * d:].astype(F32) * pc)
    out = jnp.dot(merged.astype(BF16), wo_ref[...], preferred_element_type=F32)
    xn = x_ref[...] + gate_ref[...] * out
    if final:
        ms = jnp.mean(xn * xn, axis=-1, keepdims=True)
        xn = xn * lax.rsqrt(ms + EPS) * fn_ref[...]
    o_ref[...] = xn


def _merge(xs, mods, ya, ga, u, bb, gb, yc, gc, mg, conv_w, wa, wb, wc, wo, final_norm,
           *, batch, seq_all, ctx_len, final):
    n, d = xs.shape
    tm = ROW_TILE
    tpb = seq_all // tm
    ctx_tiles = ctx_len // tm
    lat_tiles = tpb - ctx_tiles
    halo_per_tile = tm // BF16_SUBLANES
    n_halo = n // BF16_SUBLANES

    if final:
        n_steps = batch * lat_tiles

        def tile(i):
            return (i // lat_tiles) * tpb + ctx_tiles + i % lat_tiles

        def bidx(i):
            return i // lat_tiles
    else:
        n_steps = n // tm

        def tile(i):
            return i

        def bidx(i):
            return jnp.where(i % tpb < ctx_tiles, batch, i // tpb)

    def rows(width):
        return pl.BlockSpec((tm, width), lambda i: (tile(i), 0))

    def const(a):
        return pl.BlockSpec(a.shape, lambda i: (0,) * a.ndim)

    bw = u.shape[1]
    in_specs = [
        rows(d),
        pl.BlockSpec((None, 1, d), lambda i: (bidx(i) * 3 + 2, 0, 0)),
        rows(ya.shape[1]), rows(ga.shape[1]),
        rows(bw),
        pl.BlockSpec((BF16_SUBLANES, bw), lambda i: (jnp.maximum(tile(i) * halo_per_tile - 1, 0), 0)),
        pl.BlockSpec((BF16_SUBLANES, bw),
                     lambda i: (jnp.minimum((tile(i) + 1) * halo_per_tile, n_halo - 1), 0)),
        rows(bw), rows(bw),
        rows(yc.shape[1]), rows(gc.shape[1]),
        rows(mg.shape[1]),
        const(conv_w), const(wa), const(wb), const(wc), const(wo), const(final_norm),
    ]
    kwargs = {}
    if not final:
        kwargs["input_output_aliases"] = {0: 0}
    return pl.pallas_call(
        functools.partial(_merge_kernel, tiles_per_batch=tpb, ctx_tiles=ctx_tiles, final=final),
        out_shape=jax.ShapeDtypeStruct((n_steps * tm, d), F32),
        grid=(n_steps,),
        in_specs=in_specs,
        out_specs=pl.BlockSpec((tm, d), lambda i: (i, 0)),
        compiler_params=_compiler_params(1),
        name="merge_final" if final else "merge",
        **kwargs,
    )(xs, mods, ya, ga, u, u, u, bb, gb, yc, gc, mg, conv_w, wa, wb, wc, wo, final_norm)


def _rope_tables(seq, ctx_len):
    pos = jnp.arange(seq)
    rows = (pos // GRID_W).astype(F32)
    cols = (pos % GRID_W).astype(F32)
    inv_freq = ROPE_THETA ** (-jnp.arange(ROPE_HALF, dtype=F32) / ROPE_HALF)
    lane = jnp.arange(LANES)
    d64 = lane % HEAD_DIM
    axis = d64 // (2 * ROPE_HALF)
    freq = inv_freq[d64 % ROPE_HALF]
    ang = jnp.where(axis[None, :] == 0, rows[:, None], cols[:, None]) * freq[None, :]
    sign = jnp.where((d64 % (2 * ROPE_HALF)) < ROPE_HALF, -1.0, 1.0).astype(F32)
    cos = jnp.concatenate([jnp.ones((ctx_len, LANES), F32), jnp.cos(ang)], axis=0)
    sin = jnp.concatenate([jnp.zeros((ctx_len, LANES), F32), jnp.sin(ang) * sign[None, :]], axis=0)
    return cos, sin


def kernel(x, c, ctx, c_ctx, w_mod, b_mod, w_in, q_norm_a, k_norm_a, conv_w_b, lam_q1, lam_k1,
           lam_q2, lam_k2, subln_c, w_branch_a, w_branch_b, w_branch_c, w_out, final_norm):
    batch, seq, d = x.shape
    ctx_len = ctx.shape[1]
    depth = w_mod.shape[0]
    seq_all = ctx_len + seq
    n = batch * seq_all
    assert ctx_len % ROW_TILE == 0 and seq % ROW_TILE == 0

    mod_rows = -(-(batch + 1) // 8) * 8
    cond = jnp.concatenate(
        [c, c_ctx[None, :], jnp.zeros((mod_rows - batch - 1, d), F32)], axis=0)
    mods = _modulation(cond, w_mod.astype(BF16), b_mod[:, None, :])
    mods = mods.reshape(depth, mod_rows * 3, 1, d)

    cos_t, sin_t = _rope_tables(seq, ctx_len)
    aw = A_HEADS * HEAD_DIM
    head_of_lane = jnp.arange(aw) // HEAD_DIM
    ones_bd = (head_of_lane[:, None] == head_of_lane[None, :]).astype(BF16)

    xs = jnp.concatenate([ctx, x], axis=1).reshape(n, d)
    dims = dict(batch=batch, seq_all=seq_all, ctx_len=ctx_len)
    out = None
    for i in range(depth):
        last = i == depth - 1
        lambda_init = 0.8 - 0.6 * math.exp(-0.3 * i)
        qg = (jnp.tile(q_norm_a[i], A_HEADS) * (ATTN_SCALE * LOG2_E))[None, :]
        kg = jnp.tile(k_norm_a[i], A_KV_HEADS)[None, :]
        (qa, kd, vat, ga, u, bb, gb, qc, kc, vct, gc, mg) = _projection(
            xs, mods[i], w_in[i].astype(BF16), qg, kg, cos_t, sin_t, ones_bd, **dims)
        ya = _gqa_attention(qa, kd, vat, **dims)
        yc = _diff_attention(qc, kc, vct, lam_q1[i][None, :], lam_k1[i][None, :],
                             lam_q2[i][None, :], lam_k2[i][None, :], subln_c[i][None, :],
                             lambda_init=lambda_init, **dims)
        res = _merge(xs, mods[i], ya, ga, u, bb, gb, yc, gc, mg, conv_w_b[i],
                     w_branch_a[i].astype(BF16), w_branch_b[i].astype(BF16),
                     w_branch_c[i].astype(BF16), w_out[i].astype(BF16), final_norm[None, :],
                     final=last, **dims)
        if last:
            out = res.reshape(batch, seq, d)
        else:
            xs = res
    return out
```

```python
import functools
import math

import jax
import jax.numpy as jnp
from jax import lax
from jax.experimental import pallas as pl
from jax.experimental.pallas import tpu as pltpu

F32 = jnp.float32
BF16 = jnp.bfloat16

HEAD_DIM = 64
GRID_W = 64
ROPE_HALF = HEAD_DIM // 4
ROPE_THETA = 10000.0
ATTN_SCALE = HEAD_DIM ** -0.5
LOG2_E = math.log2(math.e)
A_HEADS = 8
A_KV_HEADS = 2
A_GROUP = A_HEADS // A_KV_HEADS
C_HEADS = 4
CONV_W = 3
EPS = 1e-6

LANES = 128
BF16_SUBLANES = 16
VMEM_LIMIT_BYTES = 56 * 1024 * 1024
ROW_TILE = 256
Q_TILE = 256
KEY_CHUNK = 256


def _sigmoid(x):
    return 0.5 * jnp.tanh(0.5 * x) + 0.5


def _silu(x):
    return x * _sigmoid(x)


def _compiler_params(n_axes):
    return pltpu.CompilerParams(
        dimension_semantics=("arbitrary",) * n_axes,
        vmem_limit_bytes=VMEM_LIMIT_BYTES,
    )


def _mod_kernel(c_ref, w_ref, b_ref, o_ref):
    s = _silu(c_ref[...])
    o_ref[...] = jnp.dot(s.astype(BF16), w_ref[...], preferred_element_type=F32) + b_ref[...]


def _modulation(cond, w_mod, b_mod):
    depth, d, d3 = w_mod.shape
    rows = cond.shape[0]
    return pl.pallas_call(
        _mod_kernel,
        out_shape=jax.ShapeDtypeStruct((depth, rows, d3), F32),
        grid=(depth,),
        in_specs=[
            pl.BlockSpec((rows, d), lambda i: (0, 0)),
            pl.BlockSpec((None, d, d3), lambda i: (i, 0, 0)),
            pl.BlockSpec((None, 1, d3), lambda i: (i, 0, 0)),
        ],
        out_specs=pl.BlockSpec((None, rows, d3), lambda i: (i, 0, 0)),
        compiler_params=_compiler_params(1),
        name="modulation",
    )(cond, w_mod, b_mod)


def _rope(y, cos, sin_signed):
    width = y.shape[1]
    lane = lax.broadcasted_iota(jnp.int32, y.shape, 1)
    first = (lane % (2 * ROPE_HALF)) < ROPE_HALF
    partner = jnp.where(first, pltpu.roll(y, width - ROPE_HALF, 1), pltpu.roll(y, ROPE_HALF, 1))
    return y * cos + partner * sin_signed


def _head_rms(y, ones_blockdiag, gain):
    ss = jnp.dot((y * y).astype(BF16), ones_blockdiag, preferred_element_type=F32)
    return y * lax.rsqrt(ss * (1.0 / HEAD_DIM) + EPS) * gain


def _adaln(x, shift, scale):
    ms = jnp.mean(x * x, axis=-1, keepdims=True)
    return (x * lax.rsqrt(ms + EPS) * (1.0 + scale) + shift).astype(BF16)


def _proj_kernel(x_ref, shift_ref, scale_ref, w_ref, qg_ref, kg_ref, cos_ref, sin_ref, ones_ref,
                 qa_ref, kd_ref, vat_ref, ga_ref, u_ref, bb_ref, gb_ref,
                 qc_ref, kc_ref, vct_ref, gc_ref, *, widths):
    aw, kvw, bw, cw = widths
    hb = _adaln(x_ref[...], shift_ref[...], scale_ref[...])

    def mm(lo, width):
        return jnp.dot(hb, w_ref[:, lo:lo + width], preferred_element_type=F32)

    cos = cos_ref[...]
    sin = sin_ref[...]
    cos_w = jnp.concatenate([cos] * (aw // LANES), axis=1)
    sin_w = jnp.concatenate([sin] * (aw // LANES), axis=1)

    col = 0
    q = _head_rms(mm(col, aw), ones_ref[...], qg_ref[...])
    qa_ref[...] = _rope(q, cos_w, sin_w).T.astype(BF16)
    col += aw
    kv = mm(col, 2 * kvw)
    k = _rope(_head_rms(kv[:, :kvw], ones_ref[:kvw, :kvw], kg_ref[...]), cos, sin)
    swapped = pltpu.roll(k, HEAD_DIM, 1)
    low = lax.broadcasted_iota(jnp.int32, k.shape, 1) < HEAD_DIM
    kd_ref[...] = jnp.concatenate(
        [jnp.where(low, k, swapped), jnp.where(low, swapped, k)], axis=1).astype(BF16)
    vat_ref[...] = kv[:, kvw:].T.astype(BF16)
    col += 2 * kvw
    ga_ref[...] = _silu(mm(col, aw)).astype(BF16)
    col += aw

    yb = mm(col, 4 * bw)
    u_ref[...] = (yb[:, 2 * bw:3 * bw] * yb[:, :bw]).astype(BF16)
    bb_ref[...] = yb[:, bw:2 * bw].astype(BF16)
    gb_ref[...] = _silu(yb[:, 3 * bw:]).astype(BF16)
    col += 4 * bw

    qc_ref[...] = (_rope(mm(col, cw), cos_w, sin_w) * (ATTN_SCALE * LOG2_E)).T.astype(BF16)
    col += cw
    kc_ref[...] = _rope(mm(col, cw), cos_w, sin_w).astype(BF16)
    col += cw
    vct_ref[...] = mm(col, cw).T.astype(BF16)
    col += cw
    gc_ref[...] = _silu(mm(col, cw)).astype(BF16)


def _mod_row_index(i, tiles_per_batch, ctx_tiles, ctx_row, which):
    bidx = jnp.where(i % tiles_per_batch < ctx_tiles, ctx_row, i // tiles_per_batch)
    return bidx * 3 + which


def _projection(xs, mods, w_in, qg, kg, cos_t, sin_t, ones_bd, *, batch, seq_all, ctx_len):
    n, d = xs.shape
    tm = ROW_TILE
    tpb = seq_all // tm
    ctx_tiles = ctx_len // tm
    aw = A_HEADS * HEAD_DIM
    kvw = A_KV_HEADS * HEAD_DIM
    bw = d // 2
    cw = C_HEADS * 2 * HEAD_DIM
    assert w_in.shape[1] == 2 * aw + 2 * kvw + 4 * bw + 4 * cw

    def mod_spec(which):
        return pl.BlockSpec(
            (None, 1, d), lambda i: (_mod_row_index(i, tpb, ctx_tiles, batch, which), 0, 0))

    def rows(width):
        return pl.BlockSpec((tm, width), lambda i: (i, 0))

    def const(shape):
        return pl.BlockSpec(shape, lambda i: (0,) * len(shape))

    def transposed(width):
        return pl.BlockSpec((None, width, tm), lambda i: (i // tpb, 0, i % tpb))

    def act(width):
        return jax.ShapeDtypeStruct((n, width), BF16)

    def act_t(width):
        return jax.ShapeDtypeStruct((batch, width, seq_all), BF16)

    out_shapes = (
        act_t(aw), act(2 * kvw), act_t(kvw), act(aw),
        act(bw), act(bw), act(bw),
        act_t(cw), act(cw), act_t(cw), act(cw),
    )
    out_specs = (
        transposed(aw), rows(2 * kvw), transposed(kvw), rows(aw),
        rows(bw), rows(bw), rows(bw),
        transposed(cw), rows(cw), transposed(cw), rows(cw),
    )
    return pl.pallas_call(
        functools.partial(_proj_kernel, widths=(aw, kvw, bw, cw)),
        out_shape=out_shapes,
        grid=(n // tm,),
        in_specs=[
            rows(d), mod_spec(0), mod_spec(1),
            pl.BlockSpec(w_in.shape, lambda i: (0, 0), pipeline_mode=pl.Buffered(1)),
            const(qg.shape), const(kg.shape),
            pl.BlockSpec((tm, LANES), lambda i: (i % tpb, 0)),
            pl.BlockSpec((tm, LANES), lambda i: (i % tpb, 0)),
            const(ones_bd.shape),
        ],
        out_specs=out_specs,
        compiler_params=_compiler_params(1),
        name="projection",
    )(xs, mods, mods, w_in, qg, kg, cos_t, sin_t, ones_bd)


def _split_halves(q_t):
    top = lax.broadcasted_iota(jnp.int32, q_t.shape, 0) < HEAD_DIM
    zero = jnp.zeros_like(q_t)
    return jnp.concatenate([jnp.where(top, q_t, zero), jnp.where(top, zero, q_t)], axis=1)


def _fold8(s):
    return jnp.max(s.reshape(s.shape[0] // 8, 8, s.shape[1]), axis=0)


def _column_max(m8):
    for shift in (4, 2, 1):
        m8 = jnp.maximum(m8, pltpu.roll(m8, shift, 0))
    return m8


def _minus_max(s, m8):
    rows, w = s.shape
    return (s.reshape(rows // 8, 8, w) - m8[None]).reshape(rows, w)


def _scores(k, q_t, s_ref, m_ref):
    s = jnp.dot(k, _split_halves(q_t), preferred_element_type=F32)
    s_ref[...] = s
    m_ref[...] = _column_max(_fold8(s))


def _softmax_pv(s_ref, m_ref, v_aug, vdim):
    p = jnp.exp2(_minus_max(s_ref[...], m_ref[...])).astype(BF16)
    oa = jnp.dot(v_aug, p, preferred_element_type=F32)
    return oa[:vdim] * (1.0 / oa[vdim:vdim + 1])


class _Stage:
    def __init__(self, s_ref, m_ref, p_ref):
        self.s, self.m, self.p = s_ref, m_ref, p_ref


def _phase(k_ref, vaug_ref, vdim, *, qk=None, exp=None, pv=None):
    if qk is not None:
        qm_t = _split_halves(qk[0])
    if exp is not None:
        m_cur = exp.m[...]
    run_max = None
    oa = None
    for c in range(k_ref.shape[0] // KEY_CHUNK):
        rows = slice(c * KEY_CHUNK, (c + 1) * KEY_CHUNK)
        if qk is not None:
            s = jnp.dot(k_ref[rows, :], qm_t, preferred_element_type=F32)
            qk[1].s[rows, :] = s
            run_max = _fold8(s) if run_max is None else jnp.maximum(run_max, _fold8(s))
        if exp is not None:
            exp.p[rows, :] = jnp.exp2(_minus_max(exp.s[rows, :], m_cur)).astype(BF16)
    if pv is not None:
        oa = jnp.dot(vaug_ref[...], pv.p[...], preferred_element_type=F32)
    if qk is not None:
        qk[1].m[...] = _column_max(run_max)
    if pv is None:
        return None
    return oa[:vdim] * (1.0 / oa[vdim:vdim + 1])


def _fill_v_aug(vaug_ref, vt_ref, vdim):
    vaug_ref[:vdim, :] = vt_ref[...]
    vaug_ref[vdim:, :] = jnp.ones((BF16_SUBLANES, vaug_ref.shape[1]), BF16)


def _gqa_kernel(q_ref, k_ref, vt_ref, o_ref, vaug_ref, s0_ref, s1_ref, m0_ref, m1_ref,
                p0_ref, p1_ref, ot_ref, sc_ref, mc_ref, *, ctx_len, tq):
    seq_all = k_ref.shape[0]
    first = ctx_len // tq
    last = seq_all // tq - 1
    st0 = _Stage(s0_ref, m0_ref, p0_ref)
    st1 = _Stage(s1_ref, m1_ref, p1_ref)
    _fill_v_aug(vaug_ref, vt_ref, HEAD_DIM)
    phase = functools.partial(_phase, k_ref, vaug_ref, HEAD_DIM)

    def q_pair(t, pair):
        return q_ref[pair * LANES:(pair + 1) * LANES, pl.ds(pl.multiple_of(t * tq, tq), tq)]

    def put(pair, o):
        ot_ref[(2 * pair) * HEAD_DIM:(2 * pair + 1) * HEAD_DIM, :] = o[:, :tq]
        ot_ref[(2 * pair + 1) * HEAD_DIM:(2 * pair + 2) * HEAD_DIM, :] = o[:, tq:]

    def write_tile(t):
        o_ref[pl.ds(pl.multiple_of(t * tq, tq), tq), :] = ot_ref[...].T.astype(BF16)

    for pair in range(A_GROUP // 2):
        _scores(k_ref[:ctx_len, :], q_pair(0, pair), sc_ref, mc_ref)
        put(pair, _softmax_pv(sc_ref, mc_ref, vaug_ref[:, :ctx_len], HEAD_DIM))
    write_tile(0)

    def second_half(t):
        put(0, phase(qk=(q_pair(jnp.minimum(t + 1, last), 0), st0), exp=st1, pv=st0))

    _scores(k_ref[...], q_pair(first, 0), s0_ref, m0_ref)
    phase(qk=(q_pair(first, 1), st1), exp=st0)
    second_half(first)

    def body(t, carry):
        put(1, phase(qk=(q_pair(t, 1), st1), exp=st0, pv=st1))
        write_tile(t - 1)
        second_half(t)
        return carry

    lax.fori_loop(first + 1, last + 1, body, 0)
    put(1, phase(pv=st1))
    write_tile(last)


def _gqa_attention(qa, kd, vat, *, batch, seq_all, ctx_len):
    n = kd.shape[0]
    tq = Q_TILE
    gw = A_GROUP * HEAD_DIM
    assert A_GROUP == 4 and ctx_len == tq
    return pl.pallas_call(
        functools.partial(_gqa_kernel, ctx_len=ctx_len, tq=tq),
        out_shape=jax.ShapeDtypeStruct((n, A_HEADS * HEAD_DIM), BF16),
        grid=(batch, A_KV_HEADS),
        in_specs=[
            pl.BlockSpec((None, gw, seq_all), lambda b, g: (b, g, 0)),
            pl.BlockSpec((seq_all, 2 * HEAD_DIM), lambda b, g: (b, g)),
            pl.BlockSpec((None, HEAD_DIM, seq_all), lambda b, g: (b, g, 0)),
        ],
        out_specs=pl.BlockSpec((seq_all, gw), lambda b, g: (b, g)),
        scratch_shapes=[
            pltpu.VMEM((HEAD_DIM + BF16_SUBLANES, seq_all), BF16),
            pltpu.VMEM((seq_all, 2 * tq), F32), pltpu.VMEM((seq_all, 2 * tq), F32),
            pltpu.VMEM((8, 2 * tq), F32), pltpu.VMEM((8, 2 * tq), F32),
            pltpu.VMEM((seq_all, 2 * tq), BF16), pltpu.VMEM((seq_all, 2 * tq), BF16),
            pltpu.VMEM((gw, tq), F32),
            pltpu.VMEM((ctx_len, 2 * tq), F32), pltpu.VMEM((8, 2 * tq), F32),
        ],
        compiler_params=_compiler_params(2),
        name="gqa_attention",
    )(qa, kd, vat)


def _diff_kernel(q_ref, k_ref, vt_ref, lq1_ref, lk1_ref, lq2_ref, lk2_ref, subln_ref,
                 o_ref, vaug_ref, s0_ref, s1_ref, m0_ref, m1_ref, p0_ref, p1_ref, sc_ref, mc_ref,
                 *, ctx_len, tq, lambda_init):
    seq_all = k_ref.shape[0]
    first = ctx_len // tq
    last = seq_all // tq - 1
    n_lat = last - first + 1
    vdim = 2 * HEAD_DIM
    st0 = _Stage(s0_ref, m0_ref, p0_ref)
    st1 = _Stage(s1_ref, m1_ref, p1_ref)
    _fill_v_aug(vaug_ref, vt_ref, vdim)
    phase = functools.partial(_phase, k_ref, vaug_ref, vdim)

    lam = (jnp.exp(jnp.sum(lq1_ref[...] * lk1_ref[...], axis=-1, keepdims=True))
           - jnp.exp(jnp.sum(lq2_ref[...] * lk2_ref[...], axis=-1, keepdims=True))
           + lambda_init)
    out_gain = subln_ref[...] * (1.0 - lambda_init)

    def q_tile(t):
        return q_ref[:, pl.ds(pl.multiple_of(t * tq, tq), tq)]

    def finish(t, o):
        dlt = o[:, :tq] - lam * o[:, tq:]
        ms = jnp.mean(dlt * dlt, axis=0, keepdims=True)
        y = (dlt * lax.rsqrt(ms + EPS)).T
        o_ref[pl.ds(pl.multiple_of(t * tq, tq), tq), :] = (y * out_gain).astype(BF16)

    _scores(k_ref[:ctx_len, :], q_tile(0), sc_ref, mc_ref)
    finish(0, _softmax_pv(sc_ref, mc_ref, vaug_ref[:, :ctx_len], vdim))

    def second_half(a):
        finish(a, phase(qk=(q_tile(jnp.minimum(a + 2, last)), st0), exp=st1, pv=st0))

    _scores(k_ref[...], q_tile(first), s0_ref, m0_ref)
    phase(qk=(q_tile(first + 1), st1), exp=st0)
    second_half(first)

    def body(it, carry):
        a = first + 2 * it
        finish(a - 1, phase(qk=(q_tile(a + 1), st1), exp=st0, pv=st1))
        second_half(a)
        return carry

    lax.fori_loop(1, n_lat // 2, body, 0)
    finish(last, phase(pv=st1))


def _diff_attention(qc, kc, vct, lq1, lk1, lq2, lk2, subln, *, batch, seq_all, ctx_len, lambda_init):
    n = kc.shape[0]
    tq = Q_TILE
    vdim = 2 * HEAD_DIM
    assert ctx_len == tq and ((seq_all - ctx_len) // tq) % 2 == 0

    def vec(a):
        return pl.BlockSpec(a.shape, lambda b, h: (0, 0))

    return pl.pallas_call(
        functools.partial(_diff_kernel, ctx_len=ctx_len, tq=tq, lambda_init=lambda_init),
        out_shape=jax.ShapeDtypeStruct((n, C_HEADS * vdim), BF16),
        grid=(batch, C_HEADS),
        in_specs=[
            pl.BlockSpec((None, vdim, seq_all), lambda b, h: (b, h, 0)),
            pl.BlockSpec((seq_all, vdim), lambda b, h: (b, h)),
            pl.BlockSpec((None, vdim, seq_all), lambda b, h: (b, h, 0)),
            vec(lq1), vec(lk1), vec(lq2), vec(lk2), vec(subln),
        ],
        out_specs=pl.BlockSpec((seq_all, vdim), lambda b, h: (b, h)),
        scratch_shapes=[
            pltpu.VMEM((vdim + BF16_SUBLANES, seq_all), BF16),
            pltpu.VMEM((seq_all, 2 * tq), F32), pltpu.VMEM((seq_all, 2 * tq), F32),
            pltpu.VMEM((8, 2 * tq), F32), pltpu.VMEM((8, 2 * tq), F32),
            pltpu.VMEM((seq_all, 2 * tq), BF16), pltpu.VMEM((seq_all, 2 * tq), BF16),
            pltpu.VMEM((ctx_len, 2 * tq), F32), pltpu.VMEM((8, 2 * tq), F32),
        ],
        compiler_params=_compiler_params(2),
        name="diff_attention",
    )(qc, kc, vct, lq1, lk1, lq2, lk2, subln)


def _merge_kernel(x_ref, shift_ref, scale_ref, gate_ref, ya_ref, ga_ref, u_ref, uprev_ref, unext_ref,
                  bb_ref, gb_ref, yc_ref, gc_ref, cw_ref, wg_ref, wa_ref, wb_ref, wc_ref, wo_ref,
                  fn_ref, o_ref, *, tiles_per_batch, ctx_tiles, final):
    tm, d = x_ref.shape
    i = pl.program_id(0)
    if final:
        t = ctx_tiles + i % (tiles_per_batch - ctx_tiles)
    else:
        t = i % tiles_per_batch
    has_prev = jnp.logical_and(t != 0, t != ctx_tiles)
    has_next = jnp.logical_and(t != ctx_tiles - 1, t != tiles_per_batch - 1)

    za = ya_ref[...] * ga_ref[...]
    pa = jnp.dot(za, wa_ref[...], preferred_element_type=F32)

    u = u_ref[...].astype(F32)
    prev_row = jnp.where(has_prev, uprev_ref[...].astype(F32)[BF16_SUBLANES - 1:], 0.0)
    next_row = jnp.where(has_next, unext_ref[...].astype(F32)[:1], 0.0)
    row = lax.broadcasted_iota(jnp.int32, u.shape, 0)
    u_before = jnp.where(row == 0, prev_row, pltpu.roll(u, 1, 0))
    u_after = jnp.where(row == tm - 1, next_row, pltpu.roll(u, tm - 1, 0))
    cw = cw_ref[...]
    conv = cw[0:1] * u_before + cw[1:2] * u + cw[2:3] * u_after
    zb = (bb_ref[...].astype(F32) * conv).astype(BF16) * gb_ref[...]
    pb = jnp.dot(zb, wb_ref[...], preferred_element_type=F32)

    zc = yc_ref[...] * gc_ref[...]
    pc = jnp.dot(zc, wc_ref[...], preferred_element_type=F32)

    hb = _adaln(x_ref[...], shift_ref[...], scale_ref[...])
    mg = _sigmoid(jnp.dot(hb, wg_ref[...], preferred_element_type=F32))
    merged = mg[:, :d] * pa + mg[:, d:2 * d] * pb + mg[:, 2 * d:] * pc
    out = jnp.dot(merged.astype(BF16), wo_ref[...], preferred_element_type=F32)
    xn = x_ref[...] + gate_ref[...] * out
    if final:
        ms = jnp.mean(xn * xn, axis=-1, keepdims=True)
        xn = xn * lax.rsqrt(ms + EPS) * fn_ref[...]
    o_ref[...] = xn


def _merge(xs, mods, ya, ga, u, bb, gb, yc, gc, conv_w, wg, wa, wb, wc, wo, final_norm,
           *, batch, seq_all, ctx_len, final):
    n, d = xs.shape
    tm = ROW_TILE
    tpb = seq_all // tm
    ctx_tiles = ctx_len // tm
    lat_tiles = tpb - ctx_tiles
    halo_per_tile = tm // BF16_SUBLANES
    n_halo = n // BF16_SUBLANES

    if final:
        n_steps = batch * lat_tiles

        def tile(i):
            return (i // lat_tiles) * tpb + ctx_tiles + i % lat_tiles

        def bidx(i):
            return i // lat_tiles
    else:
        n_steps = n // tm

        def tile(i):
            return i

        def bidx(i):
            return jnp.where(i % tpb < ctx_tiles, batch, i // tpb)

    def rows(width):
        return pl.BlockSpec((tm, width), lambda i: (tile(i), 0))

    def const(a):
        return pl.BlockSpec(a.shape, lambda i: (0,) * a.ndim, pipeline_mode=pl.Buffered(1))

    def mod_spec(which):
        return pl.BlockSpec((None, 1, d), lambda i: (bidx(i) * 3 + which, 0, 0))

    bw = u.shape[1]
    in_specs = [
        rows(d), mod_spec(0), mod_spec(1), mod_spec(2),
        rows(ya.shape[1]), rows(ga.shape[1]),
        rows(bw),
        pl.BlockSpec((BF16_SUBLANES, bw), lambda i: (jnp.maximum(tile(i) * halo_per_tile - 1, 0), 0)),
        pl.BlockSpec((BF16_SUBLANES, bw),
                     lambda i: (jnp.minimum((tile(i) + 1) * halo_per_tile, n_halo - 1), 0)),
        rows(bw), rows(bw),
        rows(yc.shape[1]), rows(gc.shape[1]),
        const(conv_w), const(wg), const(wa), const(wb), const(wc), const(wo), const(final_norm),
    ]
    kwargs = {}
    if not final:
        kwargs["input_output_aliases"] = {0: 0}
    return pl.pallas_call(
        functools.partial(_merge_kernel, tiles_per_batch=tpb, ctx_tiles=ctx_tiles, final=final),
        out_shape=jax.ShapeDtypeStruct((n_steps * tm, d), F32),
        grid=(n_steps,),
        in_specs=in_specs,
        out_specs=pl.BlockSpec((tm, d), lambda i: (i, 0)),
        compiler_params=_compiler_params(1),
        name="merge_final" if final else "merge",
        **kwargs,
    )(xs, mods, mods, mods, ya, ga, u, u, u, bb, gb, yc, gc, conv_w, wg, wa, wb, wc, wo, final_norm)


def _rope_tables(seq, ctx_len):
    pos = jnp.arange(seq)
    rows = (pos // GRID_W).astype(F32)
    cols = (pos % GRID_W).astype(F32)
    inv_freq = ROPE_THETA ** (-jnp.arange(ROPE_HALF, dtype=F32) / ROPE_HALF)
    lane = jnp.arange(LANES)
    d64 = lane % HEAD_DIM
    axis = d64 // (2 * ROPE_HALF)
    freq = inv_freq[d64 % ROPE_HALF]
    ang = jnp.where(axis[None, :] == 0, rows[:, None], cols[:, None]) * freq[None, :]
    sign = jnp.where((d64 % (2 * ROPE_HALF)) < ROPE_HALF, -1.0, 1.0).astype(F32)
    cos = jnp.concatenate([jnp.ones((ctx_len, LANES), F32), jnp.cos(ang)], axis=0)
    sin = jnp.concatenate([jnp.zeros((ctx_len, LANES), F32), jnp.sin(ang) * sign[None, :]], axis=0)
    return cos, sin


def kernel(x, c, ctx, c_ctx, w_mod, b_mod, w_in, q_norm_a, k_norm_a, conv_w_b, lam_q1, lam_k1,
           lam_q2, lam_k2, subln_c, w_branch_a, w_branch_b, w_branch_c, w_out, final_norm):
    batch, seq, d = x.shape
    ctx_len = ctx.shape[1]
    depth = w_mod.shape[0]
    seq_all = ctx_len + seq
    n = batch * seq_all
    assert ctx_len % ROW_TILE == 0 and seq % ROW_TILE == 0

    mod_rows = -(-(batch + 1) // 8) * 8
    cond = jnp.concatenate(
        [c, c_ctx[None, :], jnp.zeros((mod_rows - batch - 1, d), F32)], axis=0)
    mods = _modulation(cond, w_mod.astype(BF16), b_mod[:, None, :])
    mods = mods.reshape(depth, mod_rows * 3, 1, d)

    cos_t, sin_t = _rope_tables(seq, ctx_len)
    aw = A_HEADS * HEAD_DIM
    head_of_lane = jnp.arange(aw) // HEAD_DIM
    ones_bd = (head_of_lane[:, None] == head_of_lane[None, :]).astype(BF16)

    xs = jnp.concatenate([ctx, x], axis=1).reshape(n, d)
    dims = dict(batch=batch, seq_all=seq_all, ctx_len=ctx_len)
    out = None
    for i in range(depth):
        last = i == depth - 1
        lambda_init = 0.8 - 0.6 * math.exp(-0.3 * i)
        qg = (jnp.tile(q_norm_a[i], A_HEADS) * (ATTN_SCALE * LOG2_E))[None, :]
        kg = jnp.tile(k_norm_a[i], A_KV_HEADS)[None, :]
        w_gate_cols = 3 * d
        w_proj = w_in[i, :, :-w_gate_cols].astype(BF16)
        w_gate = w_in[i, :, -w_gate_cols:].astype(BF16)
        (qa, kd, vat, ga, u, bb, gb, qc, kc, vct, gc) = _projection(
            xs, mods[i], w_proj, qg, kg, cos_t, sin_t, ones_bd, **dims)
        ya = _gqa_attention(qa, kd, vat, **dims)
        yc = _diff_attention(qc, kc, vct, lam_q1[i][None, :], lam_k1[i][None, :],
                             lam_q2[i][None, :], lam_k2[i][None, :], subln_c[i][None, :],
                             lambda_init=lambda_init, **dims)
        res = _merge(xs, mods[i], ya, ga, u, bb, gb, yc, gc, conv_w_b[i], w_gate,
                     w_branch_a[i].astype(BF16), w_branch_b[i].astype(BF16),
                     w_branch_c[i].astype(BF16), w_out[i].astype(BF16), final_norm[None, :],
                     final=last, **dims)
        if last:
            out = res.reshape(batch, seq, d)
        else:
            xs = res
    return out
```

```python
import functools
import math

import jax
import jax.numpy as jnp
from jax import lax
from jax.experimental import pallas as pl
from jax.experimental.pallas import tpu as pltpu

F32 = jnp.float32
BF16 = jnp.bfloat16

HEAD_DIM = 64
GRID_W = 64
ROPE_HALF = HEAD_DIM // 4
ROPE_THETA = 10000.0
ATTN_SCALE = HEAD_DIM ** -0.5
LOG2_E = math.log2(math.e)
A_HEADS = 8
A_KV_HEADS = 2
A_GROUP = A_HEADS // A_KV_HEADS
C_HEADS = 4
CONV_W = 3
EPS = 1e-6

LANES = 128
BF16_SUBLANES = 16
VMEM_LIMIT_BYTES = 56 * 1024 * 1024
ROW_TILE = 256
Q_TILE = 256
KEY_CHUNK = 256
SUB_CHUNK = 64


def _sigmoid(x):
    return 0.5 * jnp.tanh(0.5 * x) + 0.5


def _silu(x):
    return x * _sigmoid(x)


def _compiler_params(n_axes):
    return pltpu.CompilerParams(
        dimension_semantics=("arbitrary",) * n_axes,
        vmem_limit_bytes=VMEM_LIMIT_BYTES,
    )


def _mod_kernel(c_ref, w_ref, b_ref, o_ref):
    s = _silu(c_ref[...])
    o_ref[...] = jnp.dot(s.astype(BF16), w_ref[...], preferred_element_type=F32) + b_ref[...]


def _modulation(cond, w_mod, b_mod):
    depth, d, d3 = w_mod.shape
    rows = cond.shape[0]
    return pl.pallas_call(
        _mod_kernel,
        out_shape=jax.ShapeDtypeStruct((depth, rows, d3), F32),
        grid=(depth,),
        in_specs=[
            pl.BlockSpec((rows, d), lambda i: (0, 0)),
            pl.BlockSpec((None, d, d3), lambda i: (i, 0, 0)),
            pl.BlockSpec((None, 1, d3), lambda i: (i, 0, 0)),
        ],
        out_specs=pl.BlockSpec((None, rows, d3), lambda i: (i, 0, 0)),
        compiler_params=_compiler_params(1),
        name="modulation",
    )(cond, w_mod, b_mod)


def _rope(y, cos, sin_signed):
    width = y.shape[1]
    lane = lax.broadcasted_iota(jnp.int32, y.shape, 1)
    first = (lane % (2 * ROPE_HALF)) < ROPE_HALF
    partner = jnp.where(first, pltpu.roll(y, width - ROPE_HALF, 1), pltpu.roll(y, ROPE_HALF, 1))
    return y * cos + partner * sin_signed


def _head_rms(y, ones_blockdiag, gain):
    ss = jnp.dot((y * y).astype(BF16), ones_blockdiag, preferred_element_type=F32)
    return y * lax.rsqrt(ss * (1.0 / HEAD_DIM) + EPS) * gain


def _adaln(x, shift, scale):
    ms = jnp.mean(x * x, axis=-1, keepdims=True)
    return (x * lax.rsqrt(ms + EPS) * (1.0 + scale) + shift).astype(BF16)


def _proj_kernel(x_ref, shift_ref, scale_ref, w_ref, qg_ref, kg_ref, cos_ref, sin_ref, ones_ref,
                 qa_ref, kd_ref, vat_ref, ga_ref, u_ref, bb_ref, gb_ref,
                 qc_ref, kc_ref, vct_ref, gc_ref, *, widths):
    aw, kvw, bw, cw = widths
    hb = _adaln(x_ref[...], shift_ref[...], scale_ref[...])

    def mm(lo, width):
        return jnp.dot(hb, w_ref[:, lo:lo + width], preferred_element_type=F32)

    cos = cos_ref[...]
    sin = sin_ref[...]
    cos_w = jnp.concatenate([cos] * (aw // LANES), axis=1)
    sin_w = jnp.concatenate([sin] * (aw // LANES), axis=1)

    col = 0
    q = _head_rms(mm(col, aw), ones_ref[...], qg_ref[...])
    qa_ref[...] = _rope(q, cos_w, sin_w).T.astype(BF16)
    col += aw
    kv = mm(col, 2 * kvw)
    k = _rope(_head_rms(kv[:, :kvw], ones_ref[:kvw, :kvw], kg_ref[...]), cos, sin)
    swapped = pltpu.roll(k, HEAD_DIM, 1)
    low = lax.broadcasted_iota(jnp.int32, k.shape, 1) < HEAD_DIM
    kd_ref[...] = jnp.concatenate(
        [jnp.where(low, k, swapped), jnp.where(low, swapped, k)], axis=1).astype(BF16)
    vat_ref[...] = kv[:, kvw:].T.astype(BF16)
    col += 2 * kvw
    ga_ref[...] = _silu(mm(col, aw)).astype(BF16)
    col += aw

    yb = mm(col, 4 * bw)
    u_ref[...] = (yb[:, 2 * bw:3 * bw] * yb[:, :bw]).astype(BF16)
    bb_ref[...] = yb[:, bw:2 * bw].astype(BF16)
    gb_ref[...] = _silu(yb[:, 3 * bw:]).astype(BF16)
    col += 4 * bw

    qc_ref[...] = (_rope(mm(col, cw), cos_w, sin_w) * (ATTN_SCALE * LOG2_E)).T.astype(BF16)
    col += cw
    kc_ref[...] = _rope(mm(col, cw), cos_w, sin_w).astype(BF16)
    col += cw
    vct_ref[...] = mm(col, cw).T.astype(BF16)
    col += cw
    gc_ref[...] = _silu(mm(col, cw)).astype(BF16)


def _mod_row_index(i, tiles_per_batch, ctx_tiles, ctx_row, which):
    bidx = jnp.where(i % tiles_per_batch < ctx_tiles, ctx_row, i // tiles_per_batch)
    return bidx * 3 + which


def _projection(xs, mods, w_in, qg, kg, cos_t, sin_t, ones_bd, *, batch, seq_all, ctx_len):
    n, d = xs.shape
    tm = ROW_TILE
    tpb = seq_all // tm
    ctx_tiles = ctx_len // tm
    aw = A_HEADS * HEAD_DIM
    kvw = A_KV_HEADS * HEAD_DIM
    bw = d // 2
    cw = C_HEADS * 2 * HEAD_DIM
    assert w_in.shape[1] == 2 * aw + 2 * kvw + 4 * bw + 4 * cw

    def mod_spec(which):
        return pl.BlockSpec(
            (None, 1, d), lambda i: (_mod_row_index(i, tpb, ctx_tiles, batch, which), 0, 0))

    def rows(width):
        return pl.BlockSpec((tm, width), lambda i: (i, 0))

    def const(shape):
        return pl.BlockSpec(shape, lambda i: (0,) * len(shape))

    def transposed(width):
        return pl.BlockSpec((None, width, tm), lambda i: (i // tpb, 0, i % tpb))

    def act(width):
        return jax.ShapeDtypeStruct((n, width), BF16)

    def act_t(width):
        return jax.ShapeDtypeStruct((batch, width, seq_all), BF16)

    out_shapes = (
        act_t(aw), act(2 * kvw), act_t(kvw), act(aw),
        act(bw), act(bw), act(bw),
        act_t(cw), act(cw), act_t(cw), act(cw),
    )
    out_specs = (
        transposed(aw), rows(2 * kvw), transposed(kvw), rows(aw),
        rows(bw), rows(bw), rows(bw),
        transposed(cw), rows(cw), transposed(cw), rows(cw),
    )
    return pl.pallas_call(
        functools.partial(_proj_kernel, widths=(aw, kvw, bw, cw)),
        out_shape=out_shapes,
        grid=(n // tm,),
        in_specs=[
            rows(d), mod_spec(0), mod_spec(1),
            pl.BlockSpec(w_in.shape, lambda i: (0, 0), pipeline_mode=pl.Buffered(1)),
            const(qg.shape), const(kg.shape),
            pl.BlockSpec((tm, LANES), lambda i: (i % tpb, 0)),
            pl.BlockSpec((tm, LANES), lambda i: (i % tpb, 0)),
            const(ones_bd.shape),
        ],
        out_specs=out_specs,
        compiler_params=_compiler_params(1),
        name="projection",
    )(xs, mods, mods, w_in, qg, kg, cos_t, sin_t, ones_bd)


def _split_halves(q_t):
    top = lax.broadcasted_iota(jnp.int32, q_t.shape, 0) < HEAD_DIM
    zero = jnp.zeros_like(q_t)
    return jnp.concatenate([jnp.where(top, q_t, zero), jnp.where(top, zero, q_t)], axis=1)


def _fold8(s):
    return jnp.max(s.reshape(s.shape[0] // 8, 8, s.shape[1]), axis=0)


def _column_max(m8):
    for shift in (4, 2, 1):
        m8 = jnp.maximum(m8, pltpu.roll(m8, shift, 0))
    return m8


def _minus_max(s, m8):
    rows, w = s.shape
    return (s.reshape(rows // 8, 8, w) - m8[None]).reshape(rows, w)


def _scores(k, q_t, s_ref, m_ref):
    s = jnp.dot(k, _split_halves(q_t), preferred_element_type=F32)
    s_ref[...] = s
    m_ref[...] = _column_max(_fold8(s))


def _softmax_pv(s_ref, m_ref, v_aug, vdim):
    p = jnp.exp2(_minus_max(s_ref[...], m_ref[...])).astype(BF16)
    oa = jnp.dot(v_aug, p, preferred_element_type=F32)
    return oa[:vdim] * (1.0 / oa[vdim:vdim + 1])


class _Stage:
    def __init__(self, s_ref, base, m_ref, p_ref):
        self.s, self.base, self.m, self.p = s_ref, base, m_ref, p_ref

    def rows(self, c, j):
        start = self.base + c * KEY_CHUNK + j * SUB_CHUNK
        return pl.ds(pl.multiple_of(start, SUB_CHUNK), SUB_CHUNK)


def _phase(k_ref, vaug_ref, vdim, *, qk=None, exp=None, pv=None):
    if qk is not None:
        qm_t = _split_halves(qk[0])
    if exp is not None:
        m_cur = exp.m[...]
    run_max = None
    oa = None
    for c in range(k_ref.shape[0] // KEY_CHUNK):
        rows = slice(c * KEY_CHUNK, (c + 1) * KEY_CHUNK)
        if qk is not None:
            s = jnp.dot(k_ref[rows, :], qm_t, preferred_element_type=F32)
        for j in range(KEY_CHUNK // SUB_CHUNK):
            sub = slice(j * SUB_CHUNK, (j + 1) * SUB_CHUNK)
            if qk is not None:
                qk[1].s[qk[1].rows(c, j), :] = s[sub]
                run_max = _fold8(s[sub]) if run_max is None else jnp.maximum(run_max, _fold8(s[sub]))
            if exp is not None:
                p_rows = slice(c * KEY_CHUNK + j * SUB_CHUNK, c * KEY_CHUNK + (j + 1) * SUB_CHUNK)
                exp.p[p_rows, :] = jnp.exp2(
                    _minus_max(exp.s[exp.rows(c, j), :], m_cur)).astype(BF16)
    if pv is not None:
        oa = jnp.dot(vaug_ref[...], pv.p[...], preferred_element_type=F32)
    if qk is not None:
        qk[1].m[...] = _column_max(run_max)
    if pv is None:
        return None
    return oa[:vdim] * (1.0 / oa[vdim:vdim + 1])


def _fill_v_aug(vaug_ref, vt_ref, vdim):
    vaug_ref[:vdim, :] = vt_ref[...]
    vaug_ref[vdim:, :] = jnp.ones((BF16_SUBLANES, vaug_ref.shape[1]), BF16)


def _gqa_kernel(zero_ref, q_ref, k_ref, vt_ref, o_ref, vaug_ref, s_ref, m0_ref, m1_ref,
                p0_ref, p1_ref, ot_ref, sc_ref, mc_ref, *, ctx_len, tq):
    seq_all = k_ref.shape[0]
    first = ctx_len // tq
    last = seq_all // tq - 1
    st0 = _Stage(s_ref, zero_ref[0], m0_ref, p0_ref)
    st1 = _Stage(s_ref, zero_ref[0] + seq_all, m1_ref, p1_ref)
    _fill_v_aug(vaug_ref, vt_ref, HEAD_DIM)
    phase = functools.partial(_phase, k_ref, vaug_ref, HEAD_DIM)

    def q_pair(t, pair):
        return q_ref[pair * LANES:(pair + 1) * LANES, pl.ds(pl.multiple_of(t * tq, tq), tq)]

    def put(pair, o):
        ot_ref[(2 * pair) * HEAD_DIM:(2 * pair + 1) * HEAD_DIM, :] = o[:, :tq]
        ot_ref[(2 * pair + 1) * HEAD_DIM:(2 * pair + 2) * HEAD_DIM, :] = o[:, tq:]

    def write_tile(t):
        o_ref[pl.ds(pl.multiple_of(t * tq, tq), tq), :] = ot_ref[...].T.astype(BF16)

    for pair in range(A_GROUP // 2):
        _scores(k_ref[:ctx_len, :], q_pair(0, pair), sc_ref, mc_ref)
        put(pair, _softmax_pv(sc_ref, mc_ref, vaug_ref[:, :ctx_len], HEAD_DIM))
    write_tile(0)

    def second_half(t):
        put(0, phase(qk=(q_pair(jnp.minimum(t + 1, last), 0), st0), exp=st1, pv=st0))

    phase(qk=(q_pair(first, 0), st0))
    phase(qk=(q_pair(first, 1), st1), exp=st0)
    second_half(first)

    def body(t, carry):
        put(1, phase(qk=(q_pair(t, 1), st1), exp=st0, pv=st1))
        write_tile(t - 1)
        second_half(t)
        return carry

    lax.fori_loop(first + 1, last + 1, body, 0)
    put(1, phase(pv=st1))
    write_tile(last)


def _gqa_attention(qa, kd, vat, *, batch, seq_all, ctx_len):
    n = kd.shape[0]
    tq = Q_TILE
    gw = A_GROUP * HEAD_DIM
    assert A_GROUP == 4 and ctx_len == tq
    return pl.pallas_call(
        functools.partial(_gqa_kernel, ctx_len=ctx_len, tq=tq),
        out_shape=jax.ShapeDtypeStruct((n, A_HEADS * HEAD_DIM), BF16),
        grid=(batch, A_KV_HEADS),
        in_specs=[
            pl.BlockSpec(memory_space=pltpu.SMEM),
            pl.BlockSpec((None, gw, seq_all), lambda b, g: (b, g, 0)),
            pl.BlockSpec((seq_all, 2 * HEAD_DIM), lambda b, g: (b, g)),
            pl.BlockSpec((None, HEAD_DIM, seq_all), lambda b, g: (b, g, 0)),
        ],
        out_specs=pl.BlockSpec((seq_all, gw), lambda b, g: (b, g)),
        scratch_shapes=[
            pltpu.VMEM((HEAD_DIM + BF16_SUBLANES, seq_all), BF16),
            pltpu.VMEM((2 * seq_all, 2 * tq), F32),
            pltpu.VMEM((8, 2 * tq), F32), pltpu.VMEM((8, 2 * tq), F32),
            pltpu.VMEM((seq_all, 2 * tq), BF16), pltpu.VMEM((seq_all, 2 * tq), BF16),
            pltpu.VMEM((gw, tq), F32),
            pltpu.VMEM((ctx_len, 2 * tq), F32), pltpu.VMEM((8, 2 * tq), F32),
        ],
        compiler_params=_compiler_params(2),
        name="gqa_attention",
    )(jnp.zeros((1,), jnp.int32), qa, kd, vat)


def _diff_kernel(zero_ref, q_ref, k_ref, vt_ref, lq1_ref, lk1_ref, lq2_ref, lk2_ref, subln_ref,
                 linit_ref, o_ref, vaug_ref, s_ref, m0_ref, m1_ref, p0_ref, p1_ref, sc_ref, mc_ref,
                 *, ctx_len, tq):
    seq_all = k_ref.shape[0]
    first = ctx_len // tq
    last = seq_all // tq - 1
    n_lat = last - first + 1
    vdim = 2 * HEAD_DIM
    st0 = _Stage(s_ref, zero_ref[0], m0_ref, p0_ref)
    st1 = _Stage(s_ref, zero_ref[0] + seq_all, m1_ref, p1_ref)
    _fill_v_aug(vaug_ref, vt_ref, vdim)
    phase = functools.partial(_phase, k_ref, vaug_ref, vdim)

    lambda_init = linit_ref[...]
    lam = (jnp.exp(jnp.sum(lq1_ref[...] * lk1_ref[...], axis=-1, keepdims=True))
           - jnp.exp(jnp.sum(lq2_ref[...] * lk2_ref[...], axis=-1, keepdims=True))
           + lambda_init)
    out_gain = subln_ref[...] * (1.0 - lambda_init)

    def q_tile(t):
        return q_ref[:, pl.ds(pl.multiple_of(t * tq, tq), tq)]

    def finish(t, o):
        dlt = o[:, :tq] - lam * o[:, tq:]
        ms = jnp.mean(dlt * dlt, axis=0, keepdims=True)
        y = (dlt * lax.rsqrt(ms + EPS)).T
        o_ref[pl.ds(pl.multiple_of(t * tq, tq), tq), :] = (y * out_gain).astype(BF16)

    _scores(k_ref[:ctx_len, :], q_tile(0), sc_ref, mc_ref)
    finish(0, _softmax_pv(sc_ref, mc_ref, vaug_ref[:, :ctx_len], vdim))

    def second_half(a):
        finish(a, phase(qk=(q_tile(jnp.minimum(a + 2, last)), st0), exp=st1, pv=st0))

    phase(qk=(q_tile(first), st0))
    phase(qk=(q_tile(first + 1), st1), exp=st0)
    second_half(first)

    def body(it, carry):
        a = first + 2 * it
        finish(a - 1, phase(qk=(q_tile(a + 1), st1), exp=st0, pv=st1))
        second_half(a)
        return carry

    lax.fori_loop(1, n_lat // 2, body, 0)
    finish(last, phase(pv=st1))


def _diff_attention(qc, kc, vct, lq1, lk1, lq2, lk2, subln, *, batch, seq_all, ctx_len, lambda_init):
    n = kc.shape[0]
    linit = jnp.full((1, 1), lambda_init, F32)
    tq = Q_TILE
    vdim = 2 * HEAD_DIM
    assert ctx_len == tq and ((seq_all - ctx_len) // tq) % 2 == 0

    def vec(a):
        return pl.BlockSpec(a.shape, lambda b, h: (0, 0))

    return pl.pallas_call(
        functools.partial(_diff_kernel, ctx_len=ctx_len, tq=tq),
        out_shape=jax.ShapeDtypeStruct((n, C_HEADS * vdim), BF16),
        grid=(batch, C_HEADS),
        in_specs=[
            pl.BlockSpec(memory_space=pltpu.SMEM),
            pl.BlockSpec((None, vdim, seq_all), lambda b, h: (b, h, 0)),
            pl.BlockSpec((seq_all, vdim), lambda b, h: (b, h)),
            pl.BlockSpec((None, vdim, seq_all), lambda b, h: (b, h, 0)),
            vec(lq1), vec(lk1), vec(lq2), vec(lk2), vec(subln), vec(linit),
        ],
        out_specs=pl.BlockSpec((seq_all, vdim), lambda b, h: (b, h)),
        scratch_shapes=[
            pltpu.VMEM((vdim + BF16_SUBLANES, seq_all), BF16),
            pltpu.VMEM((2 * seq_all, 2 * tq), F32),
            pltpu.VMEM((8, 2 * tq), F32), pltpu.VMEM((8, 2 * tq), F32),
            pltpu.VMEM((seq_all, 2 * tq), BF16), pltpu.VMEM((seq_all, 2 * tq), BF16),
            pltpu.VMEM((ctx_len, 2 * tq), F32), pltpu.VMEM((8, 2 * tq), F32),
        ],
        compiler_params=_compiler_params(2),
        name="diff_attention",
    )(jnp.zeros((1,), jnp.int32), qc, kc, vct, lq1, lk1, lq2, lk2, subln, linit)


def _merge_kernel(x_ref, shift_ref, scale_ref, gate_ref, ya_ref, ga_ref, u_ref, uprev_ref, unext_ref,
                  bb_ref, gb_ref, yc_ref, gc_ref, cw_ref, wg_ref, wa_ref, wb_ref, wc_ref, wo_ref,
                  fn_ref, o_ref, *, tiles_per_batch, ctx_tiles, final):
    tm, d = x_ref.shape
    i = pl.program_id(0)
    if final:
        t = ctx_tiles + i % (tiles_per_batch - ctx_tiles)
    else:
        t = i % tiles_per_batch
    has_prev = jnp.logical_and(t != 0, t != ctx_tiles)
    has_next = jnp.logical_and(t != ctx_tiles - 1, t != tiles_per_batch - 1)

    za = ya_ref[...] * ga_ref[...]
    pa = jnp.dot(za, wa_ref[...], preferred_element_type=F32)

    u = u_ref[...].astype(F32)
    prev_row = jnp.where(has_prev, uprev_ref[...].astype(F32)[BF16_SUBLANES - 1:], 0.0)
    next_row = jnp.where(has_next, unext_ref[...].astype(F32)[:1], 0.0)
    row = lax.broadcasted_iota(jnp.int32, u.shape, 0)
    u_before = jnp.where(row == 0, prev_row, pltpu.roll(u, 1, 0))
    u_after = jnp.where(row == tm - 1, next_row, pltpu.roll(u, tm - 1, 0))
    cw = cw_ref[...]
    conv = cw[0:1] * u_before + cw[1:2] * u + cw[2:3] * u_after
    zb = (bb_ref[...].astype(F32) * conv).astype(BF16) * gb_ref[...]
    pb = jnp.dot(zb, wb_ref[...], preferred_element_type=F32)

    zc = yc_ref[...] * gc_ref[...]
    pc = jnp.dot(zc, wc_ref[...], preferred_element_type=F32)

    hb = _adaln(x_ref[...], shift_ref[...], scale_ref[...])
    mg = _sigmoid(jnp.dot(hb, wg_ref[...], preferred_element_type=F32))
    merged = mg[:, :d] * pa + mg[:, d:2 * d] * pb + mg[:, 2 * d:] * pc
    out = jnp.dot(merged.astype(BF16), wo_ref[...], preferred_element_type=F32)
    xn = x_ref[...] + gate_ref[...] * out
    if final:
        ms = jnp.mean(xn * xn, axis=-1, keepdims=True)
        xn = xn * lax.rsqrt(ms + EPS) * fn_ref[...]
    o_ref[...] = xn


def _merge(xs, mods, ya, ga, u, bb, gb, yc, gc, conv_w, wg, wa, wb, wc, wo, final_norm,
           *, batch, seq_all, ctx_len, final):
    n, d = xs.shape
    tm = ROW_TILE
    tpb = seq_all // tm
    ctx_tiles = ctx_len // tm
    lat_tiles = tpb - ctx_tiles
    halo_per_tile = tm // BF16_SUBLANES
    n_halo = n // BF16_SUBLANES

    if final:
        n_steps = batch * lat_tiles

        def tile(i):
            return (i // lat_tiles) * tpb + ctx_tiles + i % lat_tiles

        def bidx(i):
            return i // lat_tiles
    else:
        n_steps = n // tm

        def tile(i):
            return i

        def bidx(i):
            return jnp.where(i % tpb < ctx_tiles, batch, i // tpb)

    def rows(width):
        return pl.BlockSpec((tm, width), lambda i: (tile(i), 0))

    def const(a):
        return pl.BlockSpec(a.shape, lambda i: (0,) * a.ndim, pipeline_mode=pl.Buffered(1))

    def mod_spec(which):
        return pl.BlockSpec((None, 1, d), lambda i: (bidx(i) * 3 + which, 0, 0))

    bw = u.shape[1]
    in_specs = [
        rows(d), mod_spec(0), mod_spec(1), mod_spec(2),
        rows(ya.shape[1]), rows(ga.shape[1]),
        rows(bw),
        pl.BlockSpec((BF16_SUBLANES, bw), lambda i: (jnp.maximum(tile(i) * halo_per_tile - 1, 0), 0)),
        pl.BlockSpec((BF16_SUBLANES, bw),
                     lambda i: (jnp.minimum((tile(i) + 1) * halo_per_tile, n_halo - 1), 0)),
        rows(bw), rows(bw),
        rows(yc.shape[1]), rows(gc.shape[1]),
        const(conv_w), const(wg), const(wa), const(wb), const(wc), const(wo), const(final_norm),
    ]
    kwargs = {}
    if not final:
        kwargs["input_output_aliases"] = {0: 0}
    return pl.pallas_call(
        functools.partial(_merge_kernel, tiles_per_batch=tpb, ctx_tiles=ctx_tiles, final=final),
        out_shape=jax.ShapeDtypeStruct((n_steps * tm, d), F32),
        grid=(n_steps,),
        in_specs=in_specs,
        out_specs=pl.BlockSpec((tm, d), lambda i: (i, 0)),
        compiler_params=_compiler_params(1),
        name="merge_final" if final else "merge",
        **kwargs,
    )(xs, mods, mods, mods, ya, ga, u, u, u, bb, gb, yc, gc, conv_w, wg, wa, wb, wc, wo, final_norm)


def _rope_tables(seq, ctx_len):
    pos = jnp.arange(seq)
    rows = (pos // GRID_W).astype(F32)
    cols = (pos % GRID_W).astype(F32)
    inv_freq = ROPE_THETA ** (-jnp.arange(ROPE_HALF, dtype=F32) / ROPE_HALF)
    lane = jnp.arange(LANES)
    d64 = lane % HEAD_DIM
    axis = d64 // (2 * ROPE_HALF)
    freq = inv_freq[d64 % ROPE_HALF]
    ang = jnp.where(axis[None, :] == 0, rows[:, None], cols[:, None]) * freq[None, :]
    sign = jnp.where((d64 % (2 * ROPE_HALF)) < ROPE_HALF, -1.0, 1.0).astype(F32)
    cos = jnp.concatenate([jnp.ones((ctx_len, LANES), F32), jnp.cos(ang)], axis=0)
    sin = jnp.concatenate([jnp.zeros((ctx_len, LANES), F32), jnp.sin(ang) * sign[None, :]], axis=0)
    return cos, sin


def kernel(x, c, ctx, c_ctx, w_mod, b_mod, w_in, q_norm_a, k_norm_a, conv_w_b, lam_q1, lam_k1,
           lam_q2, lam_k2, subln_c, w_branch_a, w_branch_b, w_branch_c, w_out, final_norm):
    batch, seq, d = x.shape
    ctx_len = ctx.shape[1]
    depth = w_mod.shape[0]
    seq_all = ctx_len + seq
    n = batch * seq_all
    assert ctx_len % ROW_TILE == 0 and seq % ROW_TILE == 0

    mod_rows = -(-(batch + 1) // 8) * 8
    cond = jnp.concatenate(
        [c, c_ctx[None, :], jnp.zeros((mod_rows - batch - 1, d), F32)], axis=0)
    mods = _modulation(cond, w_mod.astype(BF16), b_mod[:, None, :])
    mods = mods.reshape(depth, mod_rows * 3, 1, d)

    cos_t, sin_t = _rope_tables(seq, ctx_len)
    aw = A_HEADS * HEAD_DIM
    head_of_lane = jnp.arange(aw) // HEAD_DIM
    ones_bd = (head_of_lane[:, None] == head_of_lane[None, :]).astype(BF16)

    xs = jnp.concatenate([ctx, x], axis=1).reshape(n, d)
    dims = dict(batch=batch, seq_all=seq_all, ctx_len=ctx_len)
    out = None
    for i in range(depth):
        last = i == depth - 1
        lambda_init = 0.8 - 0.6 * math.exp(-0.3 * i)
        qg = (jnp.tile(q_norm_a[i], A_HEADS) * (ATTN_SCALE * LOG2_E))[None, :]
        kg = jnp.tile(k_norm_a[i], A_KV_HEADS)[None, :]
        w_gate_cols = 3 * d
        w_proj = w_in[i, :, :-w_gate_cols].astype(BF16)
        w_gate = w_in[i, :, -w_gate_cols:].astype(BF16)
        (qa, kd, vat, ga, u, bb, gb, qc, kc, vct, gc) = _projection(
            xs, mods[i], w_proj, qg, kg, cos_t, sin_t, ones_bd, **dims)
        ya = _gqa_attention(qa, kd, vat, **dims)
        yc = _diff_attention(qc, kc, vct, lam_q1[i][None, :], lam_k1[i][None, :],
                             lam_q2[i][None, :], lam_k2[i][None, :], subln_c[i][None, :],
                             lambda_init=lambda_init, **dims)
        res = _merge(xs, mods[i], ya, ga, u, bb, gb, yc, gc, conv_w_b[i], w_gate,
                     w_branch_a[i].astype(BF16), w_branch_b[i].astype(BF16),
                     w_branch_c[i].astype(BF16), w_out[i].astype(BF16), final_norm[None, :],
                     final=last, **dims)
        if last:
            out = res.reshape(batch, seq, d)
        else:
            xs = res
    return out
```

```python
import functools
import math

import jax
import jax.numpy as jnp
from jax import lax
from jax.experimental import pallas as pl
from jax.experimental.pallas import tpu as pltpu

F32 = jnp.float32
BF16 = jnp.bfloat16

HEAD_DIM = 64
GRID_W = 64
ROPE_HALF = HEAD_DIM // 4
ROPE_THETA = 10000.0
ATTN_SCALE = HEAD_DIM ** -0.5
LOG2_E = math.log2(math.e)
A_HEADS = 8
A_KV_HEADS = 2
A_GROUP = A_HEADS // A_KV_HEADS
C_HEADS = 4
CONV_W = 3
EPS = 1e-6

LANES = 128
BF16_SUBLANES = 16
VMEM_LIMIT_BYTES = 56 * 1024 * 1024
ROW_TILES = (384, 256)
FINAL_ROW_TILE = 256
Q_TILE = 256
KEY_CHUNK = 256


def _sigmoid(x):
    return 0.5 * jnp.tanh(0.5 * x) + 0.5


def _silu(x):
    return x * _sigmoid(x)


def _row_tile(seq_all):
    return next(tm for tm in ROW_TILES if seq_all % tm == 0)


def _ctx_split(ctx_len, tm):
    tile, row = divmod(ctx_len, tm)
    assert row % 8 == 0
    return tile, row


def _modulated(fn, x, t, split, ctx_refs, batch_refs):
    tile_c, r0 = split
    below = [jnp.where(t < tile_c, c[...], b[...]) for c, b in zip(ctx_refs, batch_refs)]
    if r0 == 0:
        return fn(x, *below)
    above = [jnp.where(t <= tile_c, c[...], b[...]) for c, b in zip(ctx_refs, batch_refs)]
    return jnp.concatenate([fn(x[:r0], *above), fn(x[r0:], *below)], axis=0)


def _zero_row(a, r, cond):
    lo = (r // 8) * 8
    sub = lax.broadcasted_iota(jnp.int32, (8, a.shape[1]), 0)
    slab = jnp.where(cond, jnp.where(sub == r - lo, 0.0, a[lo:lo + 8]), a[lo:lo + 8])
    return jnp.concatenate([a[:lo], slab, a[lo + 8:]], axis=0)


def _compiler_params(n_axes):
    return pltpu.CompilerParams(
        dimension_semantics=("arbitrary",) * n_axes,
        vmem_limit_bytes=VMEM_LIMIT_BYTES,
    )


def _mod_kernel(c_ref, w_ref, b_ref, o_ref):
    s = _silu(c_ref[...])
    o_ref[...] = jnp.dot(s.astype(BF16), w_ref[...], preferred_element_type=F32) + b_ref[...]


def _modulation(cond, w_mod, b_mod):
    depth, d, d3 = w_mod.shape
    rows = cond.shape[0]
    return pl.pallas_call(
        _mod_kernel,
        out_shape=jax.ShapeDtypeStruct((depth, rows, d3), F32),
        grid=(depth,),
        in_specs=[
            pl.BlockSpec((rows, d), lambda i: (0, 0)),
            pl.BlockSpec((None, d, d3), lambda i: (i, 0, 0)),
            pl.BlockSpec((None, 1, d3), lambda i: (i, 0, 0)),
        ],
        out_specs=pl.BlockSpec((None, rows, d3), lambda i: (i, 0, 0)),
        compiler_params=_compiler_params(1),
        name="modulation",
    )(cond, w_mod, b_mod)


def _rope(y, cos, sin_signed):
    width = y.shape[1]
    lane = lax.broadcasted_iota(jnp.int32, y.shape, 1)
    first = (lane % (2 * ROPE_HALF)) < ROPE_HALF
    partner = jnp.where(first, pltpu.roll(y, width - ROPE_HALF, 1), pltpu.roll(y, ROPE_HALF, 1))
    return y * cos + partner * sin_signed


def _head_rms(y, ones_blockdiag, gain):
    ss = jnp.dot((y * y).astype(BF16), ones_blockdiag, preferred_element_type=F32)
    return y * lax.rsqrt(ss * (1.0 / HEAD_DIM) + EPS) * gain


def _adaln(x, t, split, shift_refs, scale_refs):
    ms = jnp.mean(x * x, axis=-1, keepdims=True)
    xn = x * lax.rsqrt(ms + EPS)
    h = _modulated(lambda rows, shift, scale: rows * (1.0 + scale) + shift, xn, t, split,
                   (shift_refs[0], scale_refs[0]), (shift_refs[1], scale_refs[1]))
    return h.astype(BF16)


def _proj_kernel(x_ref, shift_ref, scale_ref, shiftc_ref, scalec_ref, w_ref, qg_ref, kg_ref,
                 cos_ref, sin_ref, ones_ref,
                 qa_ref, kd_ref, vat_ref, ga_ref, u_ref, bb_ref, gb_ref,
                 qc_ref, kc_ref, vct_ref, gc_ref, *, widths, tiles_per_batch, ctx_len):
    aw, kvw, bw, cw = widths
    hb = _adaln(x_ref[...], pl.program_id(0) % tiles_per_batch, _ctx_split(ctx_len, x_ref.shape[0]),
                (shiftc_ref, shift_ref), (scalec_ref, scale_ref))

    def mm(lo, width):
        return jnp.dot(hb, w_ref[:, lo:lo + width], preferred_element_type=F32)

    cos = cos_ref[...]
    sin = sin_ref[...]
    cos_w = jnp.concatenate([cos] * (aw // LANES), axis=1)
    sin_w = jnp.concatenate([sin] * (aw // LANES), axis=1)

    col = 0
    q = _head_rms(mm(col, aw), ones_ref[...], qg_ref[...])
    qa_ref[...] = _rope(q, cos_w, sin_w).T.astype(BF16)
    col += aw
    kv = mm(col, 2 * kvw)
    k = _rope(_head_rms(kv[:, :kvw], ones_ref[:kvw, :kvw], kg_ref[...]), cos, sin)
    swapped = pltpu.roll(k, HEAD_DIM, 1)
    low = lax.broadcasted_iota(jnp.int32, k.shape, 1) < HEAD_DIM
    kd_ref[...] = jnp.concatenate(
        [jnp.where(low, k, swapped), jnp.where(low, swapped, k)], axis=1).astype(BF16)
    vat_ref[...] = kv[:, kvw:].T.astype(BF16)
    col += 2 * kvw
    ga_ref[...] = _silu(mm(col, aw)).astype(BF16)
    col += aw

    yb = mm(col, 4 * bw)
    u_ref[...] = (yb[:, 2 * bw:3 * bw] * yb[:, :bw]).astype(BF16)
    bb_ref[...] = yb[:, bw:2 * bw].astype(BF16)
    gb_ref[...] = _silu(yb[:, 3 * bw:]).astype(BF16)
    col += 4 * bw

    qc_ref[...] = (_rope(mm(col, cw), cos_w, sin_w) * (ATTN_SCALE * LOG2_E)).T.astype(BF16)
    col += cw
    kc_ref[...] = _rope(mm(col, cw), cos_w, sin_w).astype(BF16)
    col += cw
    vct_ref[...] = mm(col, cw).T.astype(BF16)
    col += cw
    gc_ref[...] = _silu(mm(col, cw)).astype(BF16)


def _projection(xs, mods, w_in, qg, kg, cos_t, sin_t, ones_bd, *, batch, seq_all, ctx_len):
    n, d = xs.shape
    tm = _row_tile(seq_all)
    tpb = seq_all // tm
    aw = A_HEADS * HEAD_DIM
    kvw = A_KV_HEADS * HEAD_DIM
    bw = d // 2
    cw = C_HEADS * 2 * HEAD_DIM
    assert w_in.shape[1] == 2 * aw + 2 * kvw + 4 * bw + 4 * cw

    def mod_spec(which):
        return pl.BlockSpec((None, 1, d), lambda i: ((i // tpb) * 3 + which, 0, 0))

    def ctx_mod_spec(which):
        return pl.BlockSpec((None, 1, d), lambda i: (batch * 3 + which, 0, 0))

    def rows(width):
        return pl.BlockSpec((tm, width), lambda i: (i, 0))

    def const(shape):
        return pl.BlockSpec(shape, lambda i: (0,) * len(shape))

    def transposed(width):
        return pl.BlockSpec((None, width, tm), lambda i: (i // tpb, 0, i % tpb))

    def act(width):
        return jax.ShapeDtypeStruct((n, width), BF16)

    def act_t(width):
        return jax.ShapeDtypeStruct((batch, width, seq_all), BF16)

    out_shapes = (
        act_t(aw), act(2 * kvw), act_t(kvw), act(aw),
        act(bw), act(bw), act(bw),
        act_t(cw), act(cw), act_t(cw), act(cw),
    )
    out_specs = (
        transposed(aw), rows(2 * kvw), transposed(kvw), rows(aw),
        rows(bw), rows(bw), rows(bw),
        transposed(cw), rows(cw), transposed(cw), rows(cw),
    )
    return pl.pallas_call(
        functools.partial(_proj_kernel, widths=(aw, kvw, bw, cw), tiles_per_batch=tpb,
                          ctx_len=ctx_len),
        out_shape=out_shapes,
        grid=(n // tm,),
        in_specs=[
            rows(d), mod_spec(0), mod_spec(1), ctx_mod_spec(0), ctx_mod_spec(1),
            pl.BlockSpec(w_in.shape, lambda i: (0, 0), pipeline_mode=pl.Buffered(1)),
            const(qg.shape), const(kg.shape),
            pl.BlockSpec((tm, LANES), lambda i: (i % tpb, 0)),
            pl.BlockSpec((tm, LANES), lambda i: (i % tpb, 0)),
            const(ones_bd.shape),
        ],
        out_specs=out_specs,
        compiler_params=_compiler_params(1),
        name="projection",
    )(xs, mods, mods, mods, mods, w_in, qg, kg, cos_t, sin_t, ones_bd)


def _split_halves(q_t):
    top = lax.broadcasted_iota(jnp.int32, q_t.shape, 0) < HEAD_DIM
    zero = jnp.zeros_like(q_t)
    return jnp.concatenate([jnp.where(top, q_t, zero), jnp.where(top, zero, q_t)], axis=1)


def _fold8(s):
    return jnp.max(s.reshape(s.shape[0] // 8, 8, s.shape[1]), axis=0)


def _column_max(m8):
    for shift in (4, 2, 1):
        m8 = jnp.maximum(m8, pltpu.roll(m8, shift, 0))
    return m8


def _minus_max(s, m8):
    rows, w = s.shape
    return (s.reshape(rows // 8, 8, w) - m8[None]).reshape(rows, w)


def _scores(k, q_t, s_ref, m_ref):
    s = jnp.dot(k, _split_halves(q_t), preferred_element_type=F32)
    s_ref[...] = s
    m_ref[...] = _column_max(_fold8(s))


def _softmax_pv(s_ref, m_ref, v_aug, vdim):
    p = jnp.exp2(_minus_max(s_ref[...], m_ref[...])).astype(BF16)
    oa = jnp.dot(v_aug, p, preferred_element_type=F32)
    return oa[:vdim] * (1.0 / oa[vdim:vdim + 1])


class _Stage:
    def __init__(self, s_ref, m_ref, p_ref):
        self.s, self.m, self.p = s_ref, m_ref, p_ref


def _phase(k_ref, vaug_ref, vdim, *, qk=None, exp=None, pv=None):
    if qk is not None:
        qm_t = _split_halves(qk[0])
    if exp is not None:
        m_cur = exp.m[...]
    run_max = None
    oa = None
    for c in range(k_ref.shape[0] // KEY_CHUNK):
        rows = slice(c * KEY_CHUNK, (c + 1) * KEY_CHUNK)
        if qk is not None:
            s = jnp.dot(k_ref[rows, :], qm_t, preferred_element_type=F32)
            qk[1].s[rows, :] = s
            run_max = _fold8(s) if run_max is None else jnp.maximum(run_max, _fold8(s))
        if exp is not None:
            exp.p[rows, :] = jnp.exp2(_minus_max(exp.s[rows, :], m_cur)).astype(BF16)
    if pv is not None:
        oa = jnp.dot(vaug_ref[...], pv.p[...], preferred_element_type=F32)
    if qk is not None:
        qk[1].m[...] = _column_max(run_max)
    if pv is None:
        return None
    return oa[:vdim] * (1.0 / oa[vdim:vdim + 1])


def _fill_v_aug(vaug_ref, vt_ref, vdim):
    vaug_ref[:vdim, :] = vt_ref[...]
    vaug_ref[vdim:, :] = jnp.ones((BF16_SUBLANES, vaug_ref.shape[1]), BF16)


def _gqa_kernel(q_ref, k_ref, vt_ref, o_ref, vaug_ref, s0_ref, s1_ref, m0_ref, m1_ref,
                p0_ref, p1_ref, ot_ref, sc_ref, mc_ref, *, ctx_len, tq):
    seq_all = k_ref.shape[0]
    first = ctx_len // tq
    last = seq_all // tq - 1
    st0 = _Stage(s0_ref, m0_ref, p0_ref)
    st1 = _Stage(s1_ref, m1_ref, p1_ref)
    _fill_v_aug(vaug_ref, vt_ref, HEAD_DIM)
    phase = functools.partial(_phase, k_ref, vaug_ref, HEAD_DIM)

    def q_pair(t, pair):
        return q_ref[pair * LANES:(pair + 1) * LANES, pl.ds(pl.multiple_of(t * tq, tq), tq)]

    def put(pair, o):
        ot_ref[(2 * pair) * HEAD_DIM:(2 * pair + 1) * HEAD_DIM, :] = o[:, :tq]
        ot_ref[(2 * pair + 1) * HEAD_DIM:(2 * pair + 2) * HEAD_DIM, :] = o[:, tq:]

    def write_tile(t):
        o_ref[pl.ds(pl.multiple_of(t * tq, tq), tq), :] = ot_ref[...].T.astype(BF16)

    for pair in range(A_GROUP // 2):
        _scores(k_ref[:ctx_len, :], q_pair(0, pair), sc_ref, mc_ref)
        put(pair, _softmax_pv(sc_ref, mc_ref, vaug_ref[:, :ctx_len], HEAD_DIM))
    write_tile(0)

    def second_half(t):
        put(0, phase(qk=(q_pair(jnp.minimum(t + 1, last), 0), st0), exp=st1, pv=st0))

    _scores(k_ref[...], q_pair(first, 0), s0_ref, m0_ref)
    phase(qk=(q_pair(first, 1), st1), exp=st0)
    second_half(first)

    def body(t, carry):
        put(1, phase(qk=(q_pair(t, 1), st1), exp=st0, pv=st1))
        write_tile(t - 1)
        second_half(t)
        return carry

    lax.fori_loop(first + 1, last + 1, body, 0)
    put(1, phase(pv=st1))
    write_tile(last)


def _gqa_attention(qa, kd, vat, *, batch, seq_all, ctx_len):
    n = kd.shape[0]
    tq = Q_TILE
    gw = A_GROUP * HEAD_DIM
    assert A_GROUP == 4 and ctx_len == tq
    return pl.pallas_call(
        functools.partial(_gqa_kernel, ctx_len=ctx_len, tq=tq),
        out_shape=jax.ShapeDtypeStruct((n, A_HEADS * HEAD_DIM), BF16),
        grid=(batch, A_KV_HEADS),
        in_specs=[
            pl.BlockSpec((None, gw, seq_all), lambda b, g: (b, g, 0)),
            pl.BlockSpec((seq_all, 2 * HEAD_DIM), lambda b, g: (b, g)),
            pl.BlockSpec((None, HEAD_DIM, seq_all), lambda b, g: (b, g, 0)),
        ],
        out_specs=pl.BlockSpec((seq_all, gw), lambda b, g: (b, g)),
        scratch_shapes=[
            pltpu.VMEM((HEAD_DIM + BF16_SUBLANES, seq_all), BF16),
            pltpu.VMEM((seq_all, 2 * tq), F32), pltpu.VMEM((seq_all, 2 * tq), F32),
            pltpu.VMEM((8, 2 * tq), F32), pltpu.VMEM((8, 2 * tq), F32),
            pltpu.VMEM((seq_all, 2 * tq), BF16), pltpu.VMEM((seq_all, 2 * tq), BF16),
            pltpu.VMEM((gw, tq), F32),
            pltpu.VMEM((ctx_len, 2 * tq), F32), pltpu.VMEM((8, 2 * tq), F32),
        ],
        compiler_params=_compiler_params(2),
        name="gqa_attention",
    )(qa, kd, vat)


def _diff_kernel(q_ref, k_ref, vt_ref, lq1_ref, lk1_ref, lq2_ref, lk2_ref, subln_ref, linit_ref,
                 o_ref, vaug_ref, s0_ref, s1_ref, m0_ref, m1_ref, p0_ref, p1_ref, sc_ref, mc_ref,
                 *, ctx_len, tq):
    seq_all = k_ref.shape[0]
    first = ctx_len // tq
    last = seq_all // tq - 1
    n_lat = last - first + 1
    vdim = 2 * HEAD_DIM
    st0 = _Stage(s0_ref, m0_ref, p0_ref)
    st1 = _Stage(s1_ref, m1_ref, p1_ref)
    _fill_v_aug(vaug_ref, vt_ref, vdim)
    phase = functools.partial(_phase, k_ref, vaug_ref, vdim)

    lambda_init = linit_ref[...]
    lam = (jnp.exp(jnp.sum(lq1_ref[...] * lk1_ref[...], axis=-1, keepdims=True))
           - jnp.exp(jnp.sum(lq2_ref[...] * lk2_ref[...], axis=-1, keepdims=True))
           + lambda_init)
    out_gain = subln_ref[...] * (1.0 - lambda_init)

    def q_tile(t):
        return q_ref[:, pl.ds(pl.multiple_of(t * tq, tq), tq)]

    def finish(t, o):
        dlt = o[:, :tq] - lam * o[:, tq:]
        ms = jnp.mean(dlt * dlt, axis=0, keepdims=True)
        y = (dlt * lax.rsqrt(ms + EPS)).T
        o_ref[pl.ds(pl.multiple_of(t * tq, tq), tq), :] = (y * out_gain).astype(BF16)

    _scores(k_ref[:ctx_len, :], q_tile(0), sc_ref, mc_ref)
    finish(0, _softmax_pv(sc_ref, mc_ref, vaug_ref[:, :ctx_len], vdim))

    def second_half(a):
        finish(a, phase(qk=(q_tile(jnp.minimum(a + 2, last)), st0), exp=st1, pv=st0))

    _scores(k_ref[...], q_tile(first), s0_ref, m0_ref)
    phase(qk=(q_tile(first + 1), st1), exp=st0)
    second_half(first)

    def body(it, carry):
        a = first + 2 * it
        finish(a - 1, phase(qk=(q_tile(a + 1), st1), exp=st0, pv=st1))
        second_half(a)
        return carry

    lax.fori_loop(1, n_lat // 2, body, 0)
    finish(last, phase(pv=st1))


def _diff_attention(qc, kc, vct, lq1, lk1, lq2, lk2, subln, *, batch, seq_all, ctx_len, lambda_init):
    n = kc.shape[0]
    linit = jnp.full((1, 1), lambda_init, F32)
    tq = Q_TILE
    vdim = 2 * HEAD_DIM
    assert ctx_len == tq and ((seq_all - ctx_len) // tq) % 2 == 0

    def vec(a):
        return pl.BlockSpec(a.shape, lambda b, h: (0, 0))

    return pl.pallas_call(
        functools.partial(_diff_kernel, ctx_len=ctx_len, tq=tq),
        out_shape=jax.ShapeDtypeStruct((n, C_HEADS * vdim), BF16),
        grid=(batch, C_HEADS),
        in_specs=[
            pl.BlockSpec((None, vdim, seq_all), lambda b, h: (b, h, 0)),
            pl.BlockSpec((seq_all, vdim), lambda b, h: (b, h)),
            pl.BlockSpec((None, vdim, seq_all), lambda b, h: (b, h, 0)),
            vec(lq1), vec(lk1), vec(lq2), vec(lk2), vec(subln), vec(linit),
        ],
        out_specs=pl.BlockSpec((seq_all, vdim), lambda b, h: (b, h)),
        scratch_shapes=[
            pltpu.VMEM((vdim + BF16_SUBLANES, seq_all), BF16),
            pltpu.VMEM((seq_all, 2 * tq), F32), pltpu.VMEM((seq_all, 2 * tq), F32),
            pltpu.VMEM((8, 2 * tq), F32), pltpu.VMEM((8, 2 * tq), F32),
            pltpu.VMEM((seq_all, 2 * tq), BF16), pltpu.VMEM((seq_all, 2 * tq), BF16),
            pltpu.VMEM((ctx_len, 2 * tq), F32), pltpu.VMEM((8, 2 * tq), F32),
        ],
        compiler_params=_compiler_params(2),
        name="diff_attention",
    )(qc, kc, vct, lq1, lk1, lq2, lk2, subln, linit)


def _merge_kernel(x_ref, shift_ref, scale_ref, gate_ref, shiftc_ref, scalec_ref, gatec_ref,
                  ya_ref, ga_ref, u_ref, uprev_ref, unext_ref,
                  bb_ref, gb_ref, yc_ref, gc_ref, cw_ref, wg_ref, wa_ref, wb_ref, wc_ref, wo_ref,
                  fn_ref, o_ref, *, tiles_per_batch, ctx_len, final):
    tm, d = x_ref.shape
    i = pl.program_id(0)
    if final:
        first_latent = ctx_len // tm
        t = first_latent + i % (tiles_per_batch - first_latent)
    else:
        t = i % tiles_per_batch
    split = _ctx_split(ctx_len, tm)
    tile_c, r0 = split
    has_prev = t != 0
    has_next = t != tiles_per_batch - 1
    if r0 == 0:
        has_prev = jnp.logical_and(has_prev, t != tile_c)
        has_next = jnp.logical_and(has_next, t != tile_c - 1)

    za = ya_ref[...] * ga_ref[...]
    pa = jnp.dot(za, wa_ref[...], preferred_element_type=F32)

    u = u_ref[...].astype(F32)
    prev_row = jnp.where(has_prev, uprev_ref[...].astype(F32)[BF16_SUBLANES - 1:], 0.0)
    next_row = jnp.where(has_next, unext_ref[...].astype(F32)[:1], 0.0)
    row = lax.broadcasted_iota(jnp.int32, u.shape, 0)
    u_before = jnp.where(row == 0, prev_row, pltpu.roll(u, 1, 0))
    u_after = jnp.where(row == tm - 1, next_row, pltpu.roll(u, tm - 1, 0))
    if r0 != 0:
        u_before = _zero_row(u_before, r0, t == tile_c)
        u_after = _zero_row(u_after, r0 - 1, t == tile_c)
    cw = cw_ref[...]
    conv = cw[0:1] * u_before + cw[1:2] * u + cw[2:3] * u_after
    zb = (bb_ref[...].astype(F32) * conv).astype(BF16) * gb_ref[...]
    pb = jnp.dot(zb, wb_ref[...], preferred_element_type=F32)

    zc = yc_ref[...] * gc_ref[...]
    pc = jnp.dot(zc, wc_ref[...], preferred_element_type=F32)

    hb = _adaln(x_ref[...], t, split, (shiftc_ref, shift_ref), (scalec_ref, scale_ref))
    mg = _sigmoid(jnp.dot(hb, wg_ref[...], preferred_element_type=F32))
    merged = mg[:, :d] * pa + mg[:, d:2 * d] * pb + mg[:, 2 * d:] * pc
    out = jnp.dot(merged.astype(BF16), wo_ref[...], preferred_element_type=F32)
    xn = x_ref[...] + _modulated(lambda rows, gate: gate * rows, out, t, split,
                                 (gatec_ref,), (gate_ref,))
    if final:
        ms = jnp.mean(xn * xn, axis=-1, keepdims=True)
        xn = xn * lax.rsqrt(ms + EPS) * fn_ref[...]
    o_ref[...] = xn


def _merge(xs, mods, ya, ga, u, bb, gb, yc, gc, conv_w, wg, wa, wb, wc, wo, final_norm,
           *, batch, seq_all, ctx_len, final):
    n, d = xs.shape
    tm = FINAL_ROW_TILE if final else _row_tile(seq_all)
    tpb = seq_all // tm
    halo_per_tile = tm // BF16_SUBLANES
    n_halo = n // BF16_SUBLANES

    if final:
        assert ctx_len % tm == 0
        ctx_tiles = ctx_len // tm
        lat_tiles = tpb - ctx_tiles
        n_steps = batch * lat_tiles

        def tile(i):
            return (i // lat_tiles) * tpb + ctx_tiles + i % lat_tiles

        def bidx(i):
            return i // lat_tiles
    else:
        n_steps = n // tm

        def tile(i):
            return i

        def bidx(i):
            return i // tpb

    def rows(width):
        return pl.BlockSpec((tm, width), lambda i: (tile(i), 0))

    def const(a):
        return pl.BlockSpec(a.shape, lambda i: (0,) * a.ndim, pipeline_mode=pl.Buffered(1))

    def mod_spec(which):
        return pl.BlockSpec((None, 1, d), lambda i: (bidx(i) * 3 + which, 0, 0))

    def ctx_mod_spec(which):
        return pl.BlockSpec((None, 1, d), lambda i: (batch * 3 + which, 0, 0))

    bw = u.shape[1]
    in_specs = [
        rows(d), mod_spec(0), mod_spec(1), mod_spec(2),
        ctx_mod_spec(0), ctx_mod_spec(1), ctx_mod_spec(2),
        rows(ya.shape[1]), rows(ga.shape[1]),
        rows(bw),
        pl.BlockSpec((BF16_SUBLANES, bw), lambda i: (jnp.maximum(tile(i) * halo_per_tile - 1, 0), 0)),
        pl.BlockSpec((BF16_SUBLANES, bw),
                     lambda i: (jnp.minimum((tile(i) + 1) * halo_per_tile, n_halo - 1), 0)),
        rows(bw), rows(bw),
        rows(yc.shape[1]), rows(gc.shape[1]),
        const(conv_w), const(wg), const(wa), const(wb), const(wc), const(wo), const(final_norm),
    ]
    kwargs = {}
    if not final:
        kwargs["input_output_aliases"] = {0: 0}
    return pl.pallas_call(
        functools.partial(_merge_kernel, tiles_per_batch=tpb, ctx_len=ctx_len, final=final),
        out_shape=jax.ShapeDtypeStruct((n_steps * tm, d), F32),
        grid=(n_steps,),
        in_specs=in_specs,
        out_specs=pl.BlockSpec((tm, d), lambda i: (i, 0)),
        compiler_params=_compiler_params(1),
        name="merge_final" if final else "merge",
        **kwargs,
    )(xs, mods, mods, mods, mods, mods, mods, ya, ga, u, u, u, bb, gb, yc, gc,
      conv_w, wg, wa, wb, wc, wo, final_norm)


def _rope_tables(seq, ctx_len):
    pos = jnp.arange(seq)
    rows = (pos // GRID_W).astype(F32)
    cols = (pos % GRID_W).astype(F32)
    inv_freq = ROPE_THETA ** (-jnp.arange(ROPE_HALF, dtype=F32) / ROPE_HALF)
    lane = jnp.arange(LANES)
    d64 = lane % HEAD_DIM
    axis = d64 // (2 * ROPE_HALF)
    freq = inv_freq[d64 % ROPE_HALF]
    ang = jnp.where(axis[None, :] == 0, rows[:, None], cols[:, None]) * freq[None, :]
    sign = jnp.where((d64 % (2 * ROPE_HALF)) < ROPE_HALF, -1.0, 1.0).astype(F32)
    cos = jnp.concatenate([jnp.ones((ctx_len, LANES), F32), jnp.cos(ang)], axis=0)
    sin = jnp.concatenate([jnp.zeros((ctx_len, LANES), F32), jnp.sin(ang) * sign[None, :]], axis=0)
    return cos, sin


def kernel(x, c, ctx, c_ctx, w_mod, b_mod, w_in, q_norm_a, k_norm_a, conv_w_b, lam_q1, lam_k1,
           lam_q2, lam_k2, subln_c, w_branch_a, w_branch_b, w_branch_c, w_out, final_norm):
    batch, seq, d = x.shape
    ctx_len = ctx.shape[1]
    depth = w_mod.shape[0]
    seq_all = ctx_len + seq
    n = batch * seq_all
    assert ctx_len % FINAL_ROW_TILE == 0 and seq % FINAL_ROW_TILE == 0

    mod_rows = -(-(batch + 1) // 8) * 8
    cond = jnp.concatenate(
        [c, c_ctx[None, :], jnp.zeros((mod_rows - batch - 1, d), F32)], axis=0)
    mods = _modulation(cond, w_mod.astype(BF16), b_mod[:, None, :])
    mods = mods.reshape(depth, mod_rows * 3, 1, d)

    cos_t, sin_t = _rope_tables(seq, ctx_len)
    aw = A_HEADS * HEAD_DIM
    head_of_lane = jnp.arange(aw) // HEAD_DIM
    ones_bd = (head_of_lane[:, None] == head_of_lane[None, :]).astype(BF16)

    xs = jnp.concatenate([ctx, x], axis=1).reshape(n, d)
    dims = dict(batch=batch, seq_all=seq_all, ctx_len=ctx_len)
    out = None
    for i in range(depth):
        last = i == depth - 1
        lambda_init = 0.8 - 0.6 * math.exp(-0.3 * i)
        qg = (jnp.tile(q_norm_a[i], A_HEADS) * (ATTN_SCALE * LOG2_E))[None, :]
        kg = jnp.tile(k_norm_a[i], A_KV_HEADS)[None, :]
        w_gate_cols = 3 * d
        w_proj = w_in[i, :, :-w_gate_cols].astype(BF16)
        w_gate = w_in[i, :, -w_gate_cols:].astype(BF16)
        (qa, kd, vat, ga, u, bb, gb, qc, kc, vct, gc) = _projection(
            xs, mods[i], w_proj, qg, kg, cos_t, sin_t, ones_bd, **dims)
        ya = _gqa_attention(qa, kd, vat, **dims)
        yc = _diff_attention(qc, kc, vct, lam_q1[i][None, :], lam_k1[i][None, :],
                             lam_q2[i][None, :], lam_k2[i][None, :], subln_c[i][None, :],
                             lambda_init=lambda_init, **dims)
        res = _merge(xs, mods[i], ya, ga, u, bb, gb, yc, gc, conv_w_b[i], w_gate,
                     w_branch_a[i].astype(BF16), w_branch_b[i].astype(BF16),
                     w_branch_c[i].astype(BF16), w_out[i].astype(BF16), final_norm[None, :],
                     final=last, **dims)
        if last:
            out = res.reshape(batch, seq, d)
        else:
            xs = res
    return out
```

```python
import functools
import math

import jax
import jax.numpy as jnp
from jax import lax
from jax.experimental import pallas as pl
from jax.experimental.pallas import tpu as pltpu

F32 = jnp.float32
BF16 = jnp.bfloat16

HEAD_DIM = 64
GRID_W = 64
ROPE_HALF = HEAD_DIM // 4
ROPE_THETA = 10000.0
ATTN_SCALE = HEAD_DIM ** -0.5
LOG2_E = math.log2(math.e)
A_HEADS = 8
A_KV_HEADS = 2
A_GROUP = A_HEADS // A_KV_HEADS
C_HEADS = 4
CONV_W = 3
EPS = 1e-6

LANES = 128
BF16_SUBLANES = 16
VMEM_LIMIT_BYTES = 56 * 1024 * 1024
ROW_TILES = (384, 256)
FINAL_ROW_TILE = 256
Q_TILE = 256
KEY_CHUNK = 256


def _sigmoid(x):
    return 0.5 * jnp.tanh(0.5 * x) + 0.5


def _silu(x):
    return x * _sigmoid(x)


def _row_tile(seq_all):
    return next(tm for tm in ROW_TILES if seq_all % tm == 0)


def _ctx_split(ctx_len, tm):
    tile, row = divmod(ctx_len, tm)
    assert row % 8 == 0
    return tile, row


def _modulated(fn, x, t, split, ctx_refs, batch_refs):
    tile_c, r0 = split
    below = [jnp.where(t < tile_c, c[...], b[...]) for c, b in zip(ctx_refs, batch_refs)]
    if r0 == 0:
        return fn(x, *below)
    above = [jnp.where(t <= tile_c, c[...], b[...]) for c, b in zip(ctx_refs, batch_refs)]
    return jnp.concatenate([fn(x[:r0], *above), fn(x[r0:], *below)], axis=0)


def _zero_row(a, r, cond):
    lo = (r // 8) * 8
    sub = lax.broadcasted_iota(jnp.int32, (8, a.shape[1]), 0)
    slab = jnp.where(cond, jnp.where(sub == r - lo, 0.0, a[lo:lo + 8]), a[lo:lo + 8])
    return jnp.concatenate([a[:lo], slab, a[lo + 8:]], axis=0)


def _compiler_params(n_axes):
    return pltpu.CompilerParams(
        dimension_semantics=("arbitrary",) * n_axes,
        vmem_limit_bytes=VMEM_LIMIT_BYTES,
    )


def _mod_kernel(c_ref, w_ref, b_ref, o_ref):
    s = _silu(c_ref[...])
    o_ref[...] = jnp.dot(s.astype(BF16), w_ref[...], preferred_element_type=F32) + b_ref[...]


def _modulation(cond, w_mod, b_mod):
    depth, d, d3 = w_mod.shape
    rows = cond.shape[0]
    return pl.pallas_call(
        _mod_kernel,
        out_shape=jax.ShapeDtypeStruct((depth, rows, d3), F32),
        grid=(depth,),
        in_specs=[
            pl.BlockSpec((rows, d), lambda i: (0, 0)),
            pl.BlockSpec((None, d, d3), lambda i: (i, 0, 0)),
            pl.BlockSpec((None, 1, d3), lambda i: (i, 0, 0)),
        ],
        out_specs=pl.BlockSpec((None, rows, d3), lambda i: (i, 0, 0)),
        compiler_params=_compiler_params(1),
        name="modulation",
    )(cond, w_mod, b_mod)


def _rope(y, cos, sin_signed):
    width = y.shape[1]
    lane = lax.broadcasted_iota(jnp.int32, y.shape, 1)
    first = (lane % (2 * ROPE_HALF)) < ROPE_HALF
    partner = jnp.where(first, pltpu.roll(y, width - ROPE_HALF, 1), pltpu.roll(y, ROPE_HALF, 1))
    return y * cos + partner * sin_signed


def _head_rms(y, ones_blockdiag, gain):
    ss = jnp.dot((y * y).astype(BF16), ones_blockdiag, preferred_element_type=F32)
    return y * lax.rsqrt(ss * (1.0 / HEAD_DIM) + EPS) * gain


def _adaln(x, t, split, shift_refs, scale_refs):
    ms = jnp.mean(x * x, axis=-1, keepdims=True)
    xn = x * lax.rsqrt(ms + EPS)
    h = _modulated(lambda rows, shift, scale: rows * (1.0 + scale) + shift, xn, t, split,
                   (shift_refs[0], scale_refs[0]), (shift_refs[1], scale_refs[1]))
    return h.astype(BF16)


def _proj_kernel(x_ref, shift_ref, scale_ref, shiftc_ref, scalec_ref, w_ref, qg_ref, kg_ref,
                 cos_ref, sin_ref, ones_ref,
                 qa_ref, kd_ref, vat_ref, ga_ref, u_ref, bb_ref, gb_ref,
                 qc_ref, kc_ref, vct_ref, gc_ref, *, widths, tiles_per_batch, ctx_len):
    aw, kvw, bw, cw = widths
    hb = _adaln(x_ref[...], pl.program_id(0) % tiles_per_batch, _ctx_split(ctx_len, x_ref.shape[0]),
                (shiftc_ref, shift_ref), (scalec_ref, scale_ref))

    def mm(lo, width):
        return jnp.dot(hb, w_ref[:, lo:lo + width], preferred_element_type=F32)

    cos = cos_ref[...]
    sin = sin_ref[...]
    cos_w = jnp.concatenate([cos] * (aw // LANES), axis=1)
    sin_w = jnp.concatenate([sin] * (aw // LANES), axis=1)

    col = 0
    q = _head_rms(mm(col, aw), ones_ref[...], qg_ref[...])
    qa_ref[...] = _rope(q, cos_w, sin_w).T.astype(BF16)
    col += aw
    kv = mm(col, 2 * kvw)
    k = _rope(_head_rms(kv[:, :kvw], ones_ref[:kvw, :kvw], kg_ref[...]), cos, sin)
    swapped = pltpu.roll(k, HEAD_DIM, 1)
    low = lax.broadcasted_iota(jnp.int32, k.shape, 1) < HEAD_DIM
    kd_ref[...] = jnp.concatenate(
        [jnp.where(low, k, swapped), jnp.where(low, swapped, k)], axis=1).astype(BF16)
    vat_ref[...] = kv[:, kvw:].T.astype(BF16)
    col += 2 * kvw
    ga_ref[...] = _silu(mm(col, aw)).astype(BF16)
    col += aw

    yb = mm(col, 4 * bw)
    u_ref[...] = (yb[:, 2 * bw:3 * bw] * yb[:, :bw]).astype(BF16)
    bb_ref[...] = yb[:, bw:2 * bw].astype(BF16)
    gb_ref[...] = _silu(yb[:, 3 * bw:]).astype(BF16)
    col += 4 * bw

    qc_ref[...] = (_rope(mm(col, cw), cos_w, sin_w) * (ATTN_SCALE * LOG2_E)).T.astype(BF16)
    col += cw
    kc_ref[...] = _rope(mm(col, cw), cos_w, sin_w).astype(BF16)
    col += cw
    vct_ref[...] = mm(col, cw).T.astype(BF16)
    col += cw
    gc_ref[...] = _silu(mm(col, cw)).astype(BF16)


def _projection(xs, mods, w_in, qg, kg, cos_t, sin_t, ones_bd, *, batch, seq_all, ctx_len):
    n, d = xs.shape
    tm = _row_tile(seq_all)
    tpb = seq_all // tm
    aw = A_HEADS * HEAD_DIM
    kvw = A_KV_HEADS * HEAD_DIM
    bw = d // 2
    cw = C_HEADS * 2 * HEAD_DIM
    assert w_in.shape[1] == 2 * aw + 2 * kvw + 4 * bw + 4 * cw

    def mod_spec(which):
        return pl.BlockSpec((None, 1, d), lambda i: ((i // tpb) * 3 + which, 0, 0))

    def ctx_mod_spec(which):
        return pl.BlockSpec((None, 1, d), lambda i: (batch * 3 + which, 0, 0))

    def rows(width):
        return pl.BlockSpec((tm, width), lambda i: (i, 0))

    def const(shape):
        return pl.BlockSpec(shape, lambda i: (0,) * len(shape))

    def transposed(width):
        return pl.BlockSpec((None, width, tm), lambda i: (i // tpb, 0, i % tpb))

    def act(width):
        return jax.ShapeDtypeStruct((n, width), BF16)

    def act_t(width):
        return jax.ShapeDtypeStruct((batch, width, seq_all), BF16)

    out_shapes = (
        act_t(aw), act(2 * kvw), act_t(kvw), act(aw),
        act(bw), act(bw), act(bw),
        act_t(cw), act(cw), act_t(cw), act(cw),
    )
    out_specs = (
        transposed(aw), rows(2 * kvw), transposed(kvw), rows(aw),
        rows(bw), rows(bw), rows(bw),
        transposed(cw), rows(cw), transposed(cw), rows(cw),
    )
    return pl.pallas_call(
        functools.partial(_proj_kernel, widths=(aw, kvw, bw, cw), tiles_per_batch=tpb,
                          ctx_len=ctx_len),
        out_shape=out_shapes,
        grid=(n // tm,),
        in_specs=[
            rows(d), mod_spec(0), mod_spec(1), ctx_mod_spec(0), ctx_mod_spec(1),
            pl.BlockSpec(w_in.shape, lambda i: (0, 0), pipeline_mode=pl.Buffered(1)),
            const(qg.shape), const(kg.shape),
            pl.BlockSpec((tm, LANES), lambda i: (i % tpb, 0)),
            pl.BlockSpec((tm, LANES), lambda i: (i % tpb, 0)),
            const(ones_bd.shape),
        ],
        out_specs=out_specs,
        compiler_params=_compiler_params(1),
        name="projection",
    )(xs, mods, mods, mods, mods, w_in, qg, kg, cos_t, sin_t, ones_bd)


def _split_halves(q_t):
    top = lax.broadcasted_iota(jnp.int32, q_t.shape, 0) < HEAD_DIM
    zero = jnp.zeros_like(q_t)
    return jnp.concatenate([jnp.where(top, q_t, zero), jnp.where(top, zero, q_t)], axis=1)


def _fold8(s):
    return jnp.max(s.reshape(s.shape[0] // 8, 8, s.shape[1]), axis=0)


def _column_max(m8):
    for shift in (4, 2, 1):
        m8 = jnp.maximum(m8, pltpu.roll(m8, shift, 0))
    return m8


def _minus_max(s, m8):
    rows, w = s.shape
    return (s.reshape(rows // 8, 8, w) - m8[None]).reshape(rows, w)


def _scores(k, q_t, s_ref, m_ref):
    s = jnp.dot(k, _split_halves(q_t), preferred_element_type=F32)
    s_ref[...] = s
    m_ref[...] = _column_max(_fold8(s))


def _softmax_pv(s_ref, m_ref, v_aug, vdim):
    p = jnp.exp2(_minus_max(s_ref[...], m_ref[...])).astype(BF16)
    oa = jnp.dot(v_aug, p, preferred_element_type=F32)
    return oa[:vdim] * (1.0 / oa[vdim:vdim + 1])


class _Stage:
    def __init__(self, s_ref, m_ref, p_ref):
        self.s, self.m, self.p = s_ref, m_ref, p_ref


def _phase(k_ref, vaug_ref, vdim, *, qk=None, exp=None, pv=None):
    if qk is not None:
        qm_t = _split_halves(qk[0])
    if exp is not None:
        m_cur = exp.m[...]
    run_max = None
    oa = None
    for c in range(k_ref.shape[0] // KEY_CHUNK):
        rows = slice(c * KEY_CHUNK, (c + 1) * KEY_CHUNK)
        if qk is not None:
            s = jnp.dot(k_ref[rows, :], qm_t, preferred_element_type=F32)
            qk[1].s[rows, :] = s
            run_max = _fold8(s) if run_max is None else jnp.maximum(run_max, _fold8(s))
        if exp is not None:
            exp.p[rows, :] = jnp.exp2(_minus_max(exp.s[rows, :], m_cur)).astype(BF16)
    if pv is not None:
        oa = jnp.dot(vaug_ref[...], pv.p[...], preferred_element_type=F32)
    if qk is not None:
        qk[1].m[...] = _column_max(run_max)
    if pv is None:
        return None
    return oa[:vdim] * (1.0 / oa[vdim:vdim + 1])


def _fill_v_aug(vaug_ref, vt_ref, vdim):
    vaug_ref[:vdim, :] = vt_ref[...]
    vaug_ref[vdim:, :] = jnp.ones((BF16_SUBLANES, vaug_ref.shape[1]), BF16)


def _gqa_kernel(q_ref, k_ref, vt_ref, o_ref, vaug_ref, s0_ref, s1_ref, m0_ref, m1_ref,
                p0_ref, p1_ref, ot_ref, sc_ref, mc_ref, *, ctx_len, tq):
    seq_all = k_ref.shape[0]
    first = ctx_len // tq
    last = seq_all // tq - 1
    st0 = _Stage(s0_ref, m0_ref, p0_ref)
    st1 = _Stage(s1_ref, m1_ref, p1_ref)
    _fill_v_aug(vaug_ref, vt_ref, HEAD_DIM)
    phase = functools.partial(_phase, k_ref, vaug_ref, HEAD_DIM)

    def q_pair(t, pair):
        return q_ref[pair * LANES:(pair + 1) * LANES, pl.ds(pl.multiple_of(t * tq, tq), tq)]

    def put(pair, o):
        ot_ref[(2 * pair) * HEAD_DIM:(2 * pair + 1) * HEAD_DIM, :] = o[:, :tq]
        ot_ref[(2 * pair + 1) * HEAD_DIM:(2 * pair + 2) * HEAD_DIM, :] = o[:, tq:]

    def write_tile(t):
        o_ref[pl.ds(pl.multiple_of(t * tq, tq), tq), :] = ot_ref[...].T.astype(BF16)

    for pair in range(A_GROUP // 2):
        _scores(k_ref[:ctx_len, :], q_pair(0, pair), sc_ref, mc_ref)
        put(pair, _softmax_pv(sc_ref, mc_ref, vaug_ref[:, :ctx_len], HEAD_DIM))
    write_tile(0)

    def second_half(t):
        put(0, phase(qk=(q_pair(jnp.minimum(t + 1, last), 0), st0), exp=st1, pv=st0))

    _scores(k_ref[...], q_pair(first, 0), s0_ref, m0_ref)
    phase(qk=(q_pair(first, 1), st1), exp=st0)
    second_half(first)

    def body(t, carry):
        put(1, phase(qk=(q_pair(t, 1), st1), exp=st0, pv=st1))
        write_tile(t - 1)
        second_half(t)
        return carry

    lax.fori_loop(first + 1, last + 1, body, 0)
    put(1, phase(pv=st1))
    write_tile(last)


def _gqa_attention(qa, kd, vat, *, batch, seq_all, ctx_len):
    n = kd.shape[0]
    tq = Q_TILE
    gw = A_GROUP * HEAD_DIM
    assert A_GROUP == 4 and ctx_len == tq
    return pl.pallas_call(
        functools.partial(_gqa_kernel, ctx_len=ctx_len, tq=tq),
        out_shape=jax.ShapeDtypeStruct((n, A_HEADS * HEAD_DIM), BF16),
        grid=(batch, A_KV_HEADS),
        in_specs=[
            pl.BlockSpec((None, gw, seq_all), lambda b, g: (b, g, 0)),
            pl.BlockSpec((seq_all, 2 * HEAD_DIM), lambda b, g: (b, g)),
            pl.BlockSpec((None, HEAD_DIM, seq_all), lambda b, g: (b, g, 0)),
        ],
        out_specs=pl.BlockSpec((seq_all, gw), lambda b, g: (b, g)),
        scratch_shapes=[
            pltpu.VMEM((HEAD_DIM + BF16_SUBLANES, seq_all), BF16),
            pltpu.VMEM((seq_all, 2 * tq), F32), pltpu.VMEM((seq_all, 2 * tq), F32),
            pltpu.VMEM((8, 2 * tq), F32), pltpu.VMEM((8, 2 * tq), F32),
            pltpu.VMEM((seq_all, 2 * tq), BF16), pltpu.VMEM((seq_all, 2 * tq), BF16),
            pltpu.VMEM((gw, tq), F32),
            pltpu.VMEM((ctx_len, 2 * tq), F32), pltpu.VMEM((8, 2 * tq), F32),
        ],
        compiler_params=_compiler_params(2),
        name="gqa_attention",
    )(qa, kd, vat)


def _diff_kernel(q_ref, k_ref, vt_ref, lq1_ref, lk1_ref, lq2_ref, lk2_ref, subln_ref, linit_ref,
                 o_ref, vaug0_ref, vaug1_ref, s0_ref, s1_ref, m0_ref, m1_ref, p0_ref, p1_ref,
                 sc_ref, mc_ref, *, ctx_len, tq):
    seq_all = k_ref.shape[0]
    first = ctx_len // tq
    last = seq_all // tq - 1
    vdim = 2 * HEAD_DIM
    stages = (_Stage(s0_ref, m0_ref, p0_ref), _Stage(s1_ref, m1_ref, p1_ref))
    vaugs = (vaug0_ref, vaug1_ref)
    phases = []
    for h in range(2):
        lanes = slice(h * vdim, (h + 1) * vdim)
        _fill_v_aug(vaugs[h], vt_ref.at[lanes, :], vdim)
        phases.append(functools.partial(_phase, k_ref.at[:, lanes], vaugs[h], vdim))

    lambda_init = linit_ref[...]
    lam = (jnp.exp(jnp.sum(lq1_ref[...] * lk1_ref[...], axis=-1, keepdims=True))
           - jnp.exp(jnp.sum(lq2_ref[...] * lk2_ref[...], axis=-1, keepdims=True))
           + lambda_init)
    out_gain = subln_ref[...] * (1.0 - lambda_init)

    def q_tile(h, t):
        return q_ref[h * vdim:(h + 1) * vdim, pl.ds(pl.multiple_of(t * tq, tq), tq)]

    def finish(h, t, o):
        dlt = o[:, :tq] - lam * o[:, tq:]
        ms = jnp.mean(dlt * dlt, axis=0, keepdims=True)
        y = (dlt * lax.rsqrt(ms + EPS)).T
        o_ref[pl.ds(pl.multiple_of(t * tq, tq), tq), h * vdim:(h + 1) * vdim] = (
            (y * out_gain).astype(BF16))

    for h in range(2):
        lanes = slice(h * vdim, (h + 1) * vdim)
        _scores(k_ref[:ctx_len, lanes], q_tile(h, 0), sc_ref, mc_ref)
        finish(h, 0, _softmax_pv(sc_ref, mc_ref, vaugs[h][:, :ctx_len], vdim))

    def second_half(t):
        finish(0, t, phases[0](qk=(q_tile(0, jnp.minimum(t + 1, last)), stages[0]),
                               exp=stages[1], pv=stages[0]))

    phases[0](qk=(q_tile(0, first), stages[0]))
    phases[1](qk=(q_tile(1, first), stages[1]), exp=stages[0])
    second_half(first)

    def body(t, carry):
        finish(1, t - 1, phases[1](qk=(q_tile(1, t), stages[1]), exp=stages[0], pv=stages[1]))
        second_half(t)
        return carry

    lax.fori_loop(first + 1, last + 1, body, 0)
    finish(1, last, phases[1](pv=stages[1]))


def _diff_attention(qc, kc, vct, lq1, lk1, lq2, lk2, subln, *, batch, seq_all, ctx_len, lambda_init):
    n = kc.shape[0]
    linit = jnp.full((1, 1), lambda_init, F32)
    tq = Q_TILE
    vdim = 2 * HEAD_DIM
    assert ctx_len == tq and C_HEADS % 2 == 0

    def vec(a):
        return pl.BlockSpec(a.shape, lambda b, h: (0, 0))

    return pl.pallas_call(
        functools.partial(_diff_kernel, ctx_len=ctx_len, tq=tq),
        out_shape=jax.ShapeDtypeStruct((n, C_HEADS * vdim), BF16),
        grid=(batch, C_HEADS // 2),
        in_specs=[
            pl.BlockSpec((None, 2 * vdim, seq_all), lambda b, h: (b, h, 0)),
            pl.BlockSpec((seq_all, 2 * vdim), lambda b, h: (b, h)),
            pl.BlockSpec((None, 2 * vdim, seq_all), lambda b, h: (b, h, 0)),
            vec(lq1), vec(lk1), vec(lq2), vec(lk2), vec(subln), vec(linit),
        ],
        out_specs=pl.BlockSpec((seq_all, 2 * vdim), lambda b, h: (b, h)),
        scratch_shapes=[
            pltpu.VMEM((vdim + BF16_SUBLANES, seq_all), BF16),
            pltpu.VMEM((vdim + BF16_SUBLANES, seq_all), BF16),
            pltpu.VMEM((seq_all, 2 * tq), F32), pltpu.VMEM((seq_all, 2 * tq), F32),
            pltpu.VMEM((8, 2 * tq), F32), pltpu.VMEM((8, 2 * tq), F32),
            pltpu.VMEM((seq_all, 2 * tq), BF16), pltpu.VMEM((seq_all, 2 * tq), BF16),
            pltpu.VMEM((ctx_len, 2 * tq), F32), pltpu.VMEM((8, 2 * tq), F32),
        ],
        compiler_params=_compiler_params(2),
        name="diff_attention",
    )(qc, kc, vct, lq1, lk1, lq2, lk2, subln, linit)


def _merge_kernel(x_ref, shift_ref, scale_ref, gate_ref, shiftc_ref, scalec_ref, gatec_ref,
                  ya_ref, ga_ref, u_ref, uprev_ref, unext_ref,
                  bb_ref, gb_ref, yc_ref, gc_ref, cw_ref, wg_ref, wa_ref, wb_ref, wc_ref, wo_ref,
                  fn_ref, o_ref, *, tiles_per_batch, ctx_len, final):
    tm, d = x_ref.shape
    i = pl.program_id(0)
    if final:
        first_latent = ctx_len // tm
        t = first_latent + i % (tiles_per_batch - first_latent)
    else:
        t = i % tiles_per_batch
    split = _ctx_split(ctx_len, tm)
    tile_c, r0 = split
    has_prev = t != 0
    has_next = t != tiles_per_batch - 1
    if r0 == 0:
        has_prev = jnp.logical_and(has_prev, t != tile_c)
        has_next = jnp.logical_and(has_next, t != tile_c - 1)

    za = ya_ref[...] * ga_ref[...]
    pa = jnp.dot(za, wa_ref[...], preferred_element_type=F32)

    u = u_ref[...].astype(F32)
    prev_row = jnp.where(has_prev, uprev_ref[...].astype(F32)[BF16_SUBLANES - 1:], 0.0)
    next_row = jnp.where(has_next, unext_ref[...].astype(F32)[:1], 0.0)
    row = lax.broadcasted_iota(jnp.int32, u.shape, 0)
    u_before = jnp.where(row == 0, prev_row, pltpu.roll(u, 1, 0))
    u_after = jnp.where(row == tm - 1, next_row, pltpu.roll(u, tm - 1, 0))
    if r0 != 0:
        u_before = _zero_row(u_before, r0, t == tile_c)
        u_after = _zero_row(u_after, r0 - 1, t == tile_c)
    cw = cw_ref[...]
    conv = cw[0:1] * u_before + cw[1:2] * u + cw[2:3] * u_after
    zb = (bb_ref[...].astype(F32) * conv).astype(BF16) * gb_ref[...]
    pb = jnp.dot(zb, wb_ref[...], preferred_element_type=F32)

    zc = yc_ref[...] * gc_ref[...]
    pc = jnp.dot(zc, wc_ref[...], preferred_element_type=F32)

    hb = _adaln(x_ref[...], t, split, (shiftc_ref, shift_ref), (scalec_ref, scale_ref))
    mg = _sigmoid(jnp.dot(hb, wg_ref[...], preferred_element_type=F32))
    merged = mg[:, :d] * pa + mg[:, d:2 * d] * pb + mg[:, 2 * d:] * pc
    out = jnp.dot(merged.astype(BF16), wo_ref[...], preferred_element_type=F32)
    xn = x_ref[...] + _modulated(lambda rows, gate: gate * rows, out, t, split,
                                 (gatec_ref,), (gate_ref,))
    if final:
        ms = jnp.mean(xn * xn, axis=-1, keepdims=True)
        xn = xn * lax.rsqrt(ms + EPS) * fn_ref[...]
    o_ref[...] = xn


def _merge(xs, mods, ya, ga, u, bb, gb, yc, gc, conv_w, wg, wa, wb, wc, wo, final_norm,
           *, batch, seq_all, ctx_len, final):
    n, d = xs.shape
    tm = FINAL_ROW_TILE if final else _row_tile(seq_all)
    tpb = seq_all // tm
    halo_per_tile = tm // BF16_SUBLANES
    n_halo = n // BF16_SUBLANES

    if final:
        assert ctx_len % tm == 0
        ctx_tiles = ctx_len // tm
        lat_tiles = tpb - ctx_tiles
        n_steps = batch * lat_tiles

        def tile(i):
            return (i // lat_tiles) * tpb + ctx_tiles + i % lat_tiles

        def bidx(i):
            return i // lat_tiles
    else:
        n_steps = n // tm

        def tile(i):
            return i

        def bidx(i):
            return i // tpb

    def rows(width):
        return pl.BlockSpec((tm, width), lambda i: (tile(i), 0))

    def const(a):
        return pl.BlockSpec(a.shape, lambda i: (0,) * a.ndim, pipeline_mode=pl.Buffered(1))

    def mod_spec(which):
        return pl.BlockSpec((None, 1, d), lambda i: (bidx(i) * 3 + which, 0, 0))

    def ctx_mod_spec(which):
        return pl.BlockSpec((None, 1, d), lambda i: (batch * 3 + which, 0, 0))

    bw = u.shape[1]
    in_specs = [
        rows(d), mod_spec(0), mod_spec(1), mod_spec(2),
        ctx_mod_spec(0), ctx_mod_spec(1), ctx_mod_spec(2),
        rows(ya.shape[1]), rows(ga.shape[1]),
        rows(bw),
        pl.BlockSpec((BF16_SUBLANES, bw), lambda i: (jnp.maximum(tile(i) * halo_per_tile - 1, 0), 0)),
        pl.BlockSpec((BF16_SUBLANES, bw),
                     lambda i: (jnp.minimum((tile(i) + 1) * halo_per_tile, n_halo - 1), 0)),
        rows(bw), rows(bw),
        rows(yc.shape[1]), rows(gc.shape[1]),
        const(conv_w), const(wg), const(wa), const(wb), const(wc), const(wo), const(final_norm),
    ]
    kwargs = {}
    if not final:
        kwargs["input_output_aliases"] = {0: 0}
    return pl.pallas_call(
        functools.partial(_merge_kernel, tiles_per_batch=tpb, ctx_len=ctx_len, final=final),
        out_shape=jax.ShapeDtypeStruct((n_steps * tm, d), F32),
        grid=(n_steps,),
        in_specs=in_specs,
        out_specs=pl.BlockSpec((tm, d), lambda i: (i, 0)),
        compiler_params=_compiler_params(1),
        name="merge_final" if final else "merge",
        **kwargs,
    )(xs, mods, mods, mods, mods, mods, mods, ya, ga, u, u, u, bb, gb, yc, gc,
      conv_w, wg, wa, wb, wc, wo, final_norm)


def _rope_tables(seq, ctx_len):
    pos = jnp.arange(seq)
    rows = (pos // GRID_W).astype(F32)
    cols = (pos % GRID_W).astype(F32)
    inv_freq = ROPE_THETA ** (-jnp.arange(ROPE_HALF, dtype=F32) / ROPE_HALF)
    lane = jnp.arange(LANES)
    d64 = lane % HEAD_DIM
    axis = d64 // (2 * ROPE_HALF)
    freq = inv_freq[d64 % ROPE_HALF]
    ang = jnp.where(axis[None, :] == 0, rows[:, None], cols[:, None]) * freq[None, :]
    sign = jnp.where((d64 % (2 * ROPE_HALF)) < ROPE_HALF, -1.0, 1.0).astype(F32)
    cos = jnp.concatenate([jnp.ones((ctx_len, LANES), F32), jnp.cos(ang)], axis=0)
    sin = jnp.concatenate([jnp.zeros((ctx_len, LANES), F32), jnp.sin(ang) * sign[None, :]], axis=0)
    return cos, sin


def kernel(x, c, ctx, c_ctx, w_mod, b_mod, w_in, q_norm_a, k_norm_a, conv_w_b, lam_q1, lam_k1,
           lam_q2, lam_k2, subln_c, w_branch_a, w_branch_b, w_branch_c, w_out, final_norm):
    batch, seq, d = x.shape
    ctx_len = ctx.shape[1]
    depth = w_mod.shape[0]
    seq_all = ctx_len + seq
    n = batch * seq_all
    assert ctx_len % FINAL_ROW_TILE == 0 and seq % FINAL_ROW_TILE == 0

    mod_rows = -(-(batch + 1) // 8) * 8
    cond = jnp.concatenate(
        [c, c_ctx[None, :], jnp.zeros((mod_rows - batch - 1, d), F32)], axis=0)
    mods = _modulation(cond, w_mod.astype(BF16), b_mod[:, None, :])
    mods = mods.reshape(depth, mod_rows * 3, 1, d)

    cos_t, sin_t = _rope_tables(seq, ctx_len)
    aw = A_HEADS * HEAD_DIM
    head_of_lane = jnp.arange(aw) // HEAD_DIM
    ones_bd = (head_of_lane[:, None] == head_of_lane[None, :]).astype(BF16)

    xs = jnp.concatenate([ctx, x], axis=1).reshape(n, d)
    dims = dict(batch=batch, seq_all=seq_all, ctx_len=ctx_len)
    out = None
    for i in range(depth):
        last = i == depth - 1
        lambda_init = 0.8 - 0.6 * math.exp(-0.3 * i)
        qg = (jnp.tile(q_norm_a[i], A_HEADS) * (ATTN_SCALE * LOG2_E))[None, :]
        kg = jnp.tile(k_norm_a[i], A_KV_HEADS)[None, :]
        w_gate_cols = 3 * d
        w_proj = w_in[i, :, :-w_gate_cols].astype(BF16)
        w_gate = w_in[i, :, -w_gate_cols:].astype(BF16)
        (qa, kd, vat, ga, u, bb, gb, qc, kc, vct, gc) = _projection(
            xs, mods[i], w_proj, qg, kg, cos_t, sin_t, ones_bd, **dims)
        ya = _gqa_attention(qa, kd, vat, **dims)
        yc = _diff_attention(qc, kc, vct, lam_q1[i][None, :], lam_k1[i][None, :],
                             lam_q2[i][None, :], lam_k2[i][None, :], subln_c[i][None, :],
                             lambda_init=lambda_init, **dims)
        res = _merge(xs, mods[i], ya, ga, u, bb, gb, yc, gc, conv_w_b[i], w_gate,
                     w_branch_a[i].astype(BF16), w_branch_b[i].astype(BF16),
                     w_branch_c[i].astype(BF16), w_out[i].astype(BF16), final_norm[None, :],
                     final=last, **dims)
        if last:
            out = res.reshape(batch, seq, d)
        else:
            xs = res
    return out
```

```python
import functools
import math

import jax
import jax.numpy as jnp
from jax import lax
from jax.experimental import pallas as pl
from jax.experimental.pallas import tpu as pltpu

F32 = jnp.float32
BF16 = jnp.bfloat16

HEAD_DIM = 64
GRID_W = 64
ROPE_HALF = HEAD_DIM // 4
ROPE_THETA = 10000.0
ATTN_SCALE = HEAD_DIM ** -0.5
LOG2_E = math.log2(math.e)
A_HEADS = 8
A_KV_HEADS = 2
A_GROUP = A_HEADS // A_KV_HEADS
C_HEADS = 4
CONV_W = 3
EPS = 1e-6

LANES = 128
BF16_SUBLANES = 16
VMEM_LIMIT_BYTES = 56 * 1024 * 1024
ROW_TILES = (768, 384, 256)
FINAL_ROW_TILE = 256
Q_TILE = 256
KEY_CHUNK = 256


def _sigmoid(x):
    return 0.5 * jnp.tanh(0.5 * x) + 0.5


def _silu(x):
    return x * _sigmoid(x)


def _row_tile(seq_all):
    return next(tm for tm in ROW_TILES if seq_all % tm == 0)


def _ctx_split(ctx_len, tm):
    tile, row = divmod(ctx_len, tm)
    assert row % 8 == 0
    return tile, row


def _modulated(fn, x, t, split, ctx_refs, batch_refs):
    tile_c, r0 = split
    below = [jnp.where(t < tile_c, c[...], b[...]) for c, b in zip(ctx_refs, batch_refs)]
    if r0 == 0:
        return fn(x, *below)
    above = [jnp.where(t <= tile_c, c[...], b[...]) for c, b in zip(ctx_refs, batch_refs)]
    return jnp.concatenate([fn(x[:r0], *above), fn(x[r0:], *below)], axis=0)


def _zero_row(a, r, cond):
    lo = (r // 8) * 8
    sub = lax.broadcasted_iota(jnp.int32, (8, a.shape[1]), 0)
    slab = jnp.where(cond, jnp.where(sub == r - lo, 0.0, a[lo:lo + 8]), a[lo:lo + 8])
    return jnp.concatenate([a[:lo], slab, a[lo + 8:]], axis=0)


def _compiler_params(n_axes):
    return pltpu.CompilerParams(
        dimension_semantics=("arbitrary",) * n_axes,
        vmem_limit_bytes=VMEM_LIMIT_BYTES,
    )


def _mod_kernel(c_ref, w_ref, b_ref, o_ref):
    s = _silu(c_ref[...])
    o_ref[...] = jnp.dot(s.astype(BF16), w_ref[...], preferred_element_type=F32) + b_ref[...]


def _modulation(cond, w_mod, b_mod):
    depth, d, d3 = w_mod.shape
    rows = cond.shape[0]
    return pl.pallas_call(
        _mod_kernel,
        out_shape=jax.ShapeDtypeStruct((depth, rows, d3), F32),
        grid=(depth,),
        in_specs=[
            pl.BlockSpec((rows, d), lambda i: (0, 0)),
            pl.BlockSpec((None, d, d3), lambda i: (i, 0, 0)),
            pl.BlockSpec((None, 1, d3), lambda i: (i, 0, 0)),
        ],
        out_specs=pl.BlockSpec((None, rows, d3), lambda i: (i, 0, 0)),
        compiler_params=_compiler_params(1),
        name="modulation",
    )(cond, w_mod, b_mod)


def _rope(y, cos, sin_signed):
    width = y.shape[1]
    lane = lax.broadcasted_iota(jnp.int32, y.shape, 1)
    first = (lane % (2 * ROPE_HALF)) < ROPE_HALF
    partner = jnp.where(first, pltpu.roll(y, width - ROPE_HALF, 1), pltpu.roll(y, ROPE_HALF, 1))
    return y * cos + partner * sin_signed


def _head_rms(y, ones_blockdiag, gain):
    ss = jnp.dot((y * y).astype(BF16), ones_blockdiag, preferred_element_type=F32)
    return y * lax.rsqrt(ss * (1.0 / HEAD_DIM) + EPS) * gain


def _adaln(x, t, split, shift_refs, scale_refs):
    ms = jnp.mean(x * x, axis=-1, keepdims=True)
    xn = x * lax.rsqrt(ms + EPS)
    h = _modulated(lambda rows, shift, scale: rows * (1.0 + scale) + shift, xn, t, split,
                   (shift_refs[0], scale_refs[0]), (shift_refs[1], scale_refs[1]))
    return h.astype(BF16)


def _proj_kernel(x_ref, shift_ref, scale_ref, shiftc_ref, scalec_ref, w_ref, qg_ref, kg_ref,
                 cos_ref, sin_ref, ones_ref,
                 qa_ref, kd_ref, vat_ref, ga_ref, u_ref, bb_ref, gb_ref,
                 qc_ref, kc_ref, vct_ref, gc_ref, *, widths, tiles_per_batch, ctx_len):
    aw, kvw, bw, cw = widths
    hb = _adaln(x_ref[...], pl.program_id(0) % tiles_per_batch, _ctx_split(ctx_len, x_ref.shape[0]),
                (shiftc_ref, shift_ref), (scalec_ref, scale_ref))

    def mm(lo, width):
        return jnp.dot(hb, w_ref[:, lo:lo + width], preferred_element_type=F32)

    cos = cos_ref[...]
    sin = sin_ref[...]
    cos_w = jnp.concatenate([cos] * (aw // LANES), axis=1)
    sin_w = jnp.concatenate([sin] * (aw // LANES), axis=1)

    col = 0
    q = _head_rms(mm(col, aw), ones_ref[...], qg_ref[...])
    qa_ref[...] = _rope(q, cos_w, sin_w).T.astype(BF16)
    col += aw
    kv = mm(col, 2 * kvw)
    k = _rope(_head_rms(kv[:, :kvw], ones_ref[:kvw, :kvw], kg_ref[...]), cos, sin)
    swapped = pltpu.roll(k, HEAD_DIM, 1)
    low = lax.broadcasted_iota(jnp.int32, k.shape, 1) < HEAD_DIM
    kd_ref[...] = jnp.concatenate(
        [jnp.where(low, k, swapped), jnp.where(low, swapped, k)], axis=1).astype(BF16)
    vat_ref[...] = kv[:, kvw:].T.astype(BF16)
    col += 2 * kvw
    ga_ref[...] = _silu(mm(col, aw)).astype(BF16)
    col += aw

    yb = mm(col, 4 * bw)
    u_ref[...] = (yb[:, 2 * bw:3 * bw] * yb[:, :bw]).astype(BF16)
    bb_ref[...] = yb[:, bw:2 * bw].astype(BF16)
    gb_ref[...] = _silu(yb[:, 3 * bw:]).astype(BF16)
    col += 4 * bw

    qc_ref[...] = (_rope(mm(col, cw), cos_w, sin_w) * (ATTN_SCALE * LOG2_E)).T.astype(BF16)
    col += cw
    kc_ref[...] = _rope(mm(col, cw), cos_w, sin_w).astype(BF16)
    col += cw
    vct_ref[...] = mm(col, cw).T.astype(BF16)
    col += cw
    gc_ref[...] = _silu(mm(col, cw)).astype(BF16)


def _projection(xs, mods, w_in, qg, kg, cos_t, sin_t, ones_bd, *, batch, seq_all, ctx_len):
    n, d = xs.shape
    tm = _row_tile(seq_all)
    tpb = seq_all // tm
    aw = A_HEADS * HEAD_DIM
    kvw = A_KV_HEADS * HEAD_DIM
    bw = d // 2
    cw = C_HEADS * 2 * HEAD_DIM
    assert w_in.shape[1] == 2 * aw + 2 * kvw + 4 * bw + 4 * cw

    def mod_spec(which):
        return pl.BlockSpec((None, 1, d), lambda i: ((i // tpb) * 3 + which, 0, 0))

    def ctx_mod_spec(which):
        return pl.BlockSpec((None, 1, d), lambda i: (batch * 3 + which, 0, 0))

    def rows(width):
        return pl.BlockSpec((tm, width), lambda i: (i, 0))

    def const(shape):
        return pl.BlockSpec(shape, lambda i: (0,) * len(shape))

    def transposed(width):
        return pl.BlockSpec((None, width, tm), lambda i: (i // tpb, 0, i % tpb))

    def act(width):
        return jax.ShapeDtypeStruct((n, width), BF16)

    def act_t(width):
        return jax.ShapeDtypeStruct((batch, width, seq_all), BF16)

    out_shapes = (
        act_t(aw), act(2 * kvw), act_t(kvw), act(aw),
        act(bw), act(bw), act(bw),
        act_t(cw), act(cw), act_t(cw), act(cw),
    )
    out_specs = (
        transposed(aw), rows(2 * kvw), transposed(kvw), rows(aw),
        rows(bw), rows(bw), rows(bw),
        transposed(cw), rows(cw), transposed(cw), rows(cw),
    )
    return pl.pallas_call(
        functools.partial(_proj_kernel, widths=(aw, kvw, bw, cw), tiles_per_batch=tpb,
                          ctx_len=ctx_len),
        out_shape=out_shapes,
        grid=(n // tm,),
        in_specs=[
            rows(d), mod_spec(0), mod_spec(1), ctx_mod_spec(0), ctx_mod_spec(1),
            pl.BlockSpec(w_in.shape, lambda i: (0, 0), pipeline_mode=pl.Buffered(1)),
            const(qg.shape), const(kg.shape),
            pl.BlockSpec((tm, LANES), lambda i: (i % tpb, 0)),
            pl.BlockSpec((tm, LANES), lambda i: (i % tpb, 0)),
            const(ones_bd.shape),
        ],
        out_specs=out_specs,
        compiler_params=_compiler_params(1),
        name="projection",
    )(xs, mods, mods, mods, mods, w_in, qg, kg, cos_t, sin_t, ones_bd)


def _split_halves(q_t):
    top = lax.broadcasted_iota(jnp.int32, q_t.shape, 0) < HEAD_DIM
    zero = jnp.zeros_like(q_t)
    return jnp.concatenate([jnp.where(top, q_t, zero), jnp.where(top, zero, q_t)], axis=1)


def _fold8(s):
    return jnp.max(s.reshape(s.shape[0] // 8, 8, s.shape[1]), axis=0)


def _column_max(m8):
    for shift in (4, 2, 1):
        m8 = jnp.maximum(m8, pltpu.roll(m8, shift, 0))
    return m8


def _minus_max(s, m8):
    rows, w = s.shape
    return (s.reshape(rows // 8, 8, w) - m8[None]).reshape(rows, w)


def _scores(k, q_t, s_ref, m_ref):
    s = jnp.dot(k, _split_halves(q_t), preferred_element_type=F32)
    s_ref[...] = s
    m_ref[...] = _column_max(_fold8(s))


def _softmax_pv(s_ref, m_ref, v_aug, vdim):
    p = jnp.exp2(_minus_max(s_ref[...], m_ref[...])).astype(BF16)
    oa = jnp.dot(v_aug, p, preferred_element_type=F32)
    return oa[:vdim] * (1.0 / oa[vdim:vdim + 1])


class _Stage:
    def __init__(self, s_ref, m_ref, p_ref):
        self.s, self.m, self.p = s_ref, m_ref, p_ref


def _phase(k_ref, vaug_ref, vdim, *, qk=None, exp=None, pv=None):
    if qk is not None:
        qm_t = _split_halves(qk[0])
    if exp is not None:
        m_cur = exp.m[...]
    run_max = None
    oa = None
    for c in range(k_ref.shape[0] // KEY_CHUNK):
        rows = slice(c * KEY_CHUNK, (c + 1) * KEY_CHUNK)
        if qk is not None:
            s = jnp.dot(k_ref[rows, :], qm_t, preferred_element_type=F32)
            qk[1].s[rows, :] = s
            run_max = _fold8(s) if run_max is None else jnp.maximum(run_max, _fold8(s))
        if exp is not None:
            exp.p[rows, :] = jnp.exp2(_minus_max(exp.s[rows, :], m_cur)).astype(BF16)
    if pv is not None:
        oa = jnp.dot(vaug_ref[...], pv.p[...], preferred_element_type=F32)
    if qk is not None:
        qk[1].m[...] = _column_max(run_max)
    if pv is None:
        return None
    return oa[:vdim] * (1.0 / oa[vdim:vdim + 1])


def _fill_v_aug(vaug_ref, vt_ref, vdim):
    vaug_ref[:vdim, :] = vt_ref[...]
    vaug_ref[vdim:, :] = jnp.ones((BF16_SUBLANES, vaug_ref.shape[1]), BF16)


def _gqa_kernel(q_ref, k_ref, vt_ref, o_ref, vaug_ref, s0_ref, s1_ref, m0_ref, m1_ref,
                p0_ref, p1_ref, ot_ref, sc_ref, mc_ref, *, ctx_len, tq):
    seq_all = k_ref.shape[0]
    first = ctx_len // tq
    last = seq_all // tq - 1
    st0 = _Stage(s0_ref, m0_ref, p0_ref)
    st1 = _Stage(s1_ref, m1_ref, p1_ref)
    _fill_v_aug(vaug_ref, vt_ref, HEAD_DIM)
    phase = functools.partial(_phase, k_ref, vaug_ref, HEAD_DIM)

    def q_pair(t, pair):
        return q_ref[pair * LANES:(pair + 1) * LANES, pl.ds(pl.multiple_of(t * tq, tq), tq)]

    def put(pair, o):
        ot_ref[(2 * pair) * HEAD_DIM:(2 * pair + 1) * HEAD_DIM, :] = o[:, :tq]
        ot_ref[(2 * pair + 1) * HEAD_DIM:(2 * pair + 2) * HEAD_DIM, :] = o[:, tq:]

    def write_tile(t):
        o_ref[pl.ds(pl.multiple_of(t * tq, tq), tq), :] = ot_ref[...].T.astype(BF16)

    for pair in range(A_GROUP // 2):
        _scores(k_ref[:ctx_len, :], q_pair(0, pair), sc_ref, mc_ref)
        put(pair, _softmax_pv(sc_ref, mc_ref, vaug_ref[:, :ctx_len], HEAD_DIM))
    write_tile(0)

    def second_half(t):
        put(0, phase(qk=(q_pair(jnp.minimum(t + 1, last), 0), st0), exp=st1, pv=st0))

    _scores(k_ref[...], q_pair(first, 0), s0_ref, m0_ref)
    phase(qk=(q_pair(first, 1), st1), exp=st0)
    second_half(first)

    def body(t, carry):
        put(1, phase(qk=(q_pair(t, 1), st1), exp=st0, pv=st1))
        write_tile(t - 1)
        second_half(t)
        return carry

    lax.fori_loop(first + 1, last + 1, body, 0)
    put(1, phase(pv=st1))
    write_tile(last)


def _gqa_attention(qa, kd, vat, *, batch, seq_all, ctx_len):
    n = kd.shape[0]
    tq = Q_TILE
    gw = A_GROUP * HEAD_DIM
    assert A_GROUP == 4 and ctx_len == tq
    return pl.pallas_call(
        functools.partial(_gqa_kernel, ctx_len=ctx_len, tq=tq),
        out_shape=jax.ShapeDtypeStruct((n, A_HEADS * HEAD_DIM), BF16),
        grid=(batch, A_KV_HEADS),
        in_specs=[
            pl.BlockSpec((None, gw, seq_all), lambda b, g: (b, g, 0)),
            pl.BlockSpec((seq_all, 2 * HEAD_DIM), lambda b, g: (b, g)),
            pl.BlockSpec((None, HEAD_DIM, seq_all), lambda b, g: (b, g, 0)),
        ],
        out_specs=pl.BlockSpec((seq_all, gw), lambda b, g: (b, g)),
        scratch_shapes=[
            pltpu.VMEM((HEAD_DIM + BF16_SUBLANES, seq_all), BF16),
            pltpu.VMEM((seq_all, 2 * tq), F32), pltpu.VMEM((seq_all, 2 * tq), F32),
            pltpu.VMEM((8, 2 * tq), F32), pltpu.VMEM((8, 2 * tq), F32),
            pltpu.VMEM((seq_all, 2 * tq), BF16), pltpu.VMEM((seq_all, 2 * tq), BF16),
            pltpu.VMEM((gw, tq), F32),
            pltpu.VMEM((ctx_len, 2 * tq), F32), pltpu.VMEM((8, 2 * tq), F32),
        ],
        compiler_params=_compiler_params(2),
        name="gqa_attention",
    )(qa, kd, vat)


def _diff_kernel(q_ref, k_ref, vt_ref, lq1_ref, lk1_ref, lq2_ref, lk2_ref, subln_ref, linit_ref,
                 o_ref, vaug0_ref, vaug1_ref, s0_ref, s1_ref, m0_ref, m1_ref, p0_ref, p1_ref,
                 sc_ref, mc_ref, *, ctx_len, tq):
    seq_all = k_ref.shape[0]
    first = ctx_len // tq
    last = seq_all // tq - 1
    vdim = 2 * HEAD_DIM
    stages = (_Stage(s0_ref, m0_ref, p0_ref), _Stage(s1_ref, m1_ref, p1_ref))
    vaugs = (vaug0_ref, vaug1_ref)
    phases = []
    for h in range(2):
        lanes = slice(h * vdim, (h + 1) * vdim)
        _fill_v_aug(vaugs[h], vt_ref.at[lanes, :], vdim)
        phases.append(functools.partial(_phase, k_ref.at[:, lanes], vaugs[h], vdim))

    lambda_init = linit_ref[...]
    lam = (jnp.exp(jnp.sum(lq1_ref[...] * lk1_ref[...], axis=-1, keepdims=True))
           - jnp.exp(jnp.sum(lq2_ref[...] * lk2_ref[...], axis=-1, keepdims=True))
           + lambda_init)
    out_gain = subln_ref[...] * (1.0 - lambda_init)

    def q_tile(h, t):
        return q_ref[h * vdim:(h + 1) * vdim, pl.ds(pl.multiple_of(t * tq, tq), tq)]

    def finish(h, t, o):
        dlt = o[:, :tq] - lam * o[:, tq:]
        ms = jnp.mean(dlt * dlt, axis=0, keepdims=True)
        y = (dlt * lax.rsqrt(ms + EPS)).T
        o_ref[pl.ds(pl.multiple_of(t * tq, tq), tq), h * vdim:(h + 1) * vdim] = (
            (y * out_gain).astype(BF16))

    for h in range(2):
        lanes = slice(h * vdim, (h + 1) * vdim)
        _scores(k_ref[:ctx_len, lanes], q_tile(h, 0), sc_ref, mc_ref)
        finish(h, 0, _softmax_pv(sc_ref, mc_ref, vaugs[h][:, :ctx_len], vdim))

    def second_half(t):
        finish(0, t, phases[0](qk=(q_tile(0, jnp.minimum(t + 1, last)), stages[0]),
                               exp=stages[1], pv=stages[0]))

    phases[0](qk=(q_tile(0, first), stages[0]))
    phases[1](qk=(q_tile(1, first), stages[1]), exp=stages[0])
    second_half(first)

    def body(t, carry):
        finish(1, t - 1, phases[1](qk=(q_tile(1, t), stages[1]), exp=stages[0], pv=stages[1]))
        second_half(t)
        return carry

    lax.fori_loop(first + 1, last + 1, body, 0)
    finish(1, last, phases[1](pv=stages[1]))


def _diff_attention(qc, kc, vct, lq1, lk1, lq2, lk2, subln, *, batch, seq_all, ctx_len, lambda_init):
    n = kc.shape[0]
    linit = jnp.full((1, 1), lambda_init, F32)
    tq = Q_TILE
    vdim = 2 * HEAD_DIM
    assert ctx_len == tq and C_HEADS % 2 == 0

    def vec(a):
        return pl.BlockSpec(a.shape, lambda b, h: (0, 0))

    return pl.pallas_call(
        functools.partial(_diff_kernel, ctx_len=ctx_len, tq=tq),
        out_shape=jax.ShapeDtypeStruct((n, C_HEADS * vdim), BF16),
        grid=(batch, C_HEADS // 2),
        in_specs=[
            pl.BlockSpec((None, 2 * vdim, seq_all), lambda b, h: (b, h, 0)),
            pl.BlockSpec((seq_all, 2 * vdim), lambda b, h: (b, h)),
            pl.BlockSpec((None, 2 * vdim, seq_all), lambda b, h: (b, h, 0)),
            vec(lq1), vec(lk1), vec(lq2), vec(lk2), vec(subln), vec(linit),
        ],
        out_specs=pl.BlockSpec((seq_all, 2 * vdim), lambda b, h: (b, h)),
        scratch_shapes=[
            pltpu.VMEM((vdim + BF16_SUBLANES, seq_all), BF16),
            pltpu.VMEM((vdim + BF16_SUBLANES, seq_all), BF16),
            pltpu.VMEM((seq_all, 2 * tq), F32), pltpu.VMEM((seq_all, 2 * tq), F32),
            pltpu.VMEM((8, 2 * tq), F32), pltpu.VMEM((8, 2 * tq), F32),
            pltpu.VMEM((seq_all, 2 * tq), BF16), pltpu.VMEM((seq_all, 2 * tq), BF16),
            pltpu.VMEM((ctx_len, 2 * tq), F32), pltpu.VMEM((8, 2 * tq), F32),
        ],
        compiler_params=_compiler_params(2),
        name="diff_attention",
    )(qc, kc, vct, lq1, lk1, lq2, lk2, subln, linit)


def _merge_kernel(x_ref, shift_ref, scale_ref, gate_ref, shiftc_ref, scalec_ref, gatec_ref,
                  ya_ref, ga_ref, u_ref, uprev_ref, unext_ref,
                  bb_ref, gb_ref, yc_ref, gc_ref, cw_ref, wg_ref, wa_ref, wb_ref, wc_ref, wo_ref,
                  fn_ref, o_ref, *, tiles_per_batch, ctx_len, final):
    tm, d = x_ref.shape
    i = pl.program_id(0)
    if final:
        first_latent = ctx_len // tm
        t = first_latent + i % (tiles_per_batch - first_latent)
    else:
        t = i % tiles_per_batch
    split = _ctx_split(ctx_len, tm)
    tile_c, r0 = split
    has_prev = t != 0
    has_next = t != tiles_per_batch - 1
    if r0 == 0:
        has_prev = jnp.logical_and(has_prev, t != tile_c)
        has_next = jnp.logical_and(has_next, t != tile_c - 1)

    za = ya_ref[...] * ga_ref[...]
    pa = jnp.dot(za, wa_ref[...], preferred_element_type=F32)

    u = u_ref[...].astype(F32)
    prev_row = jnp.where(has_prev, uprev_ref[...].astype(F32)[BF16_SUBLANES - 1:], 0.0)
    next_row = jnp.where(has_next, unext_ref[...].astype(F32)[:1], 0.0)
    row = lax.broadcasted_iota(jnp.int32, u.shape, 0)
    u_before = jnp.where(row == 0, prev_row, pltpu.roll(u, 1, 0))
    u_after = jnp.where(row == tm - 1, next_row, pltpu.roll(u, tm - 1, 0))
    if r0 != 0:
        u_before = _zero_row(u_before, r0, t == tile_c)
        u_after = _zero_row(u_after, r0 - 1, t == tile_c)
    cw = cw_ref[...]
    conv = cw[0:1] * u_before + cw[1:2] * u + cw[2:3] * u_after
    zb = (bb_ref[...].astype(F32) * conv).astype(BF16) * gb_ref[...]
    pb = jnp.dot(zb, wb_ref[...], preferred_element_type=F32)

    zc = yc_ref[...] * gc_ref[...]
    pc = jnp.dot(zc, wc_ref[...], preferred_element_type=F32)

    hb = _adaln(x_ref[...], t, split, (shiftc_ref, shift_ref), (scalec_ref, scale_ref))
    mg = _sigmoid(jnp.dot(hb, wg_ref[...], preferred_element_type=F32))
    merged = mg[:, :d] * pa + mg[:, d:2 * d] * pb + mg[:, 2 * d:] * pc
    out = jnp.dot(merged.astype(BF16), wo_ref[...], preferred_element_type=F32)
    xn = x_ref[...] + _modulated(lambda rows, gate: gate * rows, out, t, split,
                                 (gatec_ref,), (gate_ref,))
    if final:
        ms = jnp.mean(xn * xn, axis=-1, keepdims=True)
        xn = xn * lax.rsqrt(ms + EPS) * fn_ref[...]
    o_ref[...] = xn


def _merge(xs, mods, ya, ga, u, bb, gb, yc, gc, conv_w, wg, wa, wb, wc, wo, final_norm,
           *, batch, seq_all, ctx_len, final):
    n, d = xs.shape
    tm = FINAL_ROW_TILE if final else _row_tile(seq_all)
    tpb = seq_all // tm
    halo_per_tile = tm // BF16_SUBLANES
    n_halo = n // BF16_SUBLANES

    if final:
        assert ctx_len % tm == 0
        ctx_tiles = ctx_len // tm
        lat_tiles = tpb - ctx_tiles
        n_steps = batch * lat_tiles

        def tile(i):
            return (i // lat_tiles) * tpb + ctx_tiles + i % lat_tiles

        def bidx(i):
            return i // lat_tiles
    else:
        n_steps = n // tm

        def tile(i):
            return i

        def bidx(i):
            return i // tpb

    def rows(width):
        return pl.BlockSpec((tm, width), lambda i: (tile(i), 0))

    def const(a):
        return pl.BlockSpec(a.shape, lambda i: (0,) * a.ndim, pipeline_mode=pl.Buffered(1))

    def mod_spec(which):
        return pl.BlockSpec((None, 1, d), lambda i: (bidx(i) * 3 + which, 0, 0))

    def ctx_mod_spec(which):
        return pl.BlockSpec((None, 1, d), lambda i: (batch * 3 + which, 0, 0))

    bw = u.shape[1]
    in_specs = [
        rows(d), mod_spec(0), mod_spec(1), mod_spec(2),
        ctx_mod_spec(0), ctx_mod_spec(1), ctx_mod_spec(2),
        rows(ya.shape[1]), rows(ga.shape[1]),
        rows(bw),
        pl.BlockSpec((BF16_SUBLANES, bw), lambda i: (jnp.maximum(tile(i) * halo_per_tile - 1, 0), 0)),
        pl.BlockSpec((BF16_SUBLANES, bw),
                     lambda i: (jnp.minimum((tile(i) + 1) * halo_per_tile, n_halo - 1), 0)),
        rows(bw), rows(bw),
        rows(yc.shape[1]), rows(gc.shape[1]),
        const(conv_w), const(wg), const(wa), const(wb), const(wc), const(wo), const(final_norm),
    ]
    kwargs = {}
    if not final:
        kwargs["input_output_aliases"] = {0: 0}
    return pl.pallas_call(
        functools.partial(_merge_kernel, tiles_per_batch=tpb, ctx_len=ctx_len, final=final),
        out_shape=jax.ShapeDtypeStruct((n_steps * tm, d), F32),
        grid=(n_steps,),
        in_specs=in_specs,
        out_specs=pl.BlockSpec((tm, d), lambda i: (i, 0)),
        compiler_params=_compiler_params(1),
        name="merge_final" if final else "merge",
        **kwargs,
    )(xs, mods, mods, mods, mods, mods, mods, ya, ga, u, u, u, bb, gb, yc, gc,
      conv_w, wg, wa, wb, wc, wo, final_norm)


def _rope_tables(seq, ctx_len):
    pos = jnp.arange(seq)
    rows = (pos // GRID_W).astype(F32)
    cols = (pos % GRID_W).astype(F32)
    inv_freq = ROPE_THETA ** (-jnp.arange(ROPE_HALF, dtype=F32) / ROPE_HALF)
    lane = jnp.arange(LANES)
    d64 = lane % HEAD_DIM
    axis = d64 // (2 * ROPE_HALF)
    freq = inv_freq[d64 % ROPE_HALF]
    ang = jnp.where(axis[None, :] == 0, rows[:, None], cols[:, None]) * freq[None, :]
    sign = jnp.where((d64 % (2 * ROPE_HALF)) < ROPE_HALF, -1.0, 1.0).astype(F32)
    cos = jnp.concatenate([jnp.ones((ctx_len, LANES), F32), jnp.cos(ang)], axis=0)
    sin = jnp.concatenate([jnp.zeros((ctx_len, LANES), F32), jnp.sin(ang) * sign[None, :]], axis=0)
    return cos, sin


def kernel(x, c, ctx, c_ctx, w_mod, b_mod, w_in, q_norm_a, k_norm_a, conv_w_b, lam_q1, lam_k1,
           lam_q2, lam_k2, subln_c, w_branch_a, w_branch_b, w_branch_c, w_out, final_norm):
    batch, seq, d = x.shape
    ctx_len = ctx.shape[1]
    depth = w_mod.shape[0]
    seq_all = ctx_len + seq
    n = batch * seq_all
    assert ctx_len % FINAL_ROW_TILE == 0 and seq % FINAL_ROW_TILE == 0

    mod_rows = -(-(batch + 1) // 8) * 8
    cond = jnp.concatenate(
        [c, c_ctx[None, :], jnp.zeros((mod_rows - batch - 1, d), F32)], axis=0)
    mods = _modulation(cond, w_mod.astype(BF16), b_mod[:, None, :])
    mods = mods.reshape(depth, mod_rows * 3, 1, d)

    cos_t, sin_t = _rope_tables(seq, ctx_len)
    aw = A_HEADS * HEAD_DIM
    head_of_lane = jnp.arange(aw) // HEAD_DIM
    ones_bd = (head_of_lane[:, None] == head_of_lane[None, :]).astype(BF16)

    xs = jnp.concatenate([ctx, x], axis=1).reshape(n, d)
    dims = dict(batch=batch, seq_all=seq_all, ctx_len=ctx_len)
    out = None
    for i in range(depth):
        last = i == depth - 1
        lambda_init = 0.8 - 0.6 * math.exp(-0.3 * i)
        qg = (jnp.tile(q_norm_a[i], A_HEADS) * (ATTN_SCALE * LOG2_E))[None, :]
        kg = jnp.tile(k_norm_a[i], A_KV_HEADS)[None, :]
        w_gate_cols = 3 * d
        w_proj = w_in[i, :, :-w_gate_cols].astype(BF16)
        w_gate = w_in[i, :, -w_gate_cols:].astype(BF16)
        (qa, kd, vat, ga, u, bb, gb, qc, kc, vct, gc) = _projection(
            xs, mods[i], w_proj, qg, kg, cos_t, sin_t, ones_bd, **dims)
        ya = _gqa_attention(qa, kd, vat, **dims)
        yc = _diff_attention(qc, kc, vct, lam_q1[i][None, :], lam_k1[i][None, :],
                             lam_q2[i][None, :], lam_k2[i][None, :], subln_c[i][None, :],
                             lambda_init=lambda_init, **dims)
        res = _merge(xs, mods[i], ya, ga, u, bb, gb, yc, gc, conv_w_b[i], w_gate,
                     w_branch_a[i].astype(BF16), w_branch_b[i].astype(BF16),
                     w_branch_c[i].astype(BF16), w_out[i].astype(BF16), final_norm[None, :],
                     final=last, **dims)
        if last:
            out = res.reshape(batch, seq, d)
        else:
            xs = res
    return out
```
